```python
import math
import jax, jax.numpy as jnp
from jax import lax
import numpy as np

D_MODEL = 2048
BATCH = 8
SEQ = 2048
DEPTH = 2

CHUNK = 64
Q_BLOCK = 128
N_MIXERS = 2
N_ATTN = (DEPTH + 1) // 2
N_RWKV = DEPTH // 2

DA_HEADS = 8
DA_HEAD_DIM = D_MODEL // DA_HEADS // 2
DA_V_DIM = 2 * DA_HEAD_DIM
ROT_DIM = DA_HEAD_DIM // 4
ROPE_THETA = 500000.0
DA_SUBLN_EPS = 1e-5

RW_HEAD = 64
RW_HEADS = D_MODEL // RW_HEAD
LORA_DECAY = 96
LORA_A = 96
LORA_GATE = 256
GN_EPS = 64e-5

FFN = 4 * D_MODEL
EPS = 1e-6
NEG_INF = -1e30

kernel_name = "hybrid_diffattn_rwkv7_sqrelu_stream"


def rms_norm(x, g, eps=EPS):
    xf = x.astype(jnp.float32)
    y = xf * lax.rsqrt(jnp.mean(xf * xf, axis=-1, keepdims=True) + eps)
    return (y * g.astype(jnp.float32)).astype(x.dtype)


def rope_partial(x, pos):
    half = ROT_DIM // 2
    inv_freq = ROPE_THETA ** (-jnp.arange(half, dtype=jnp.float32) * (2.0 / ROT_DIM))
    ang = pos[:, None] * inv_freq[None, :]
    cos = jnp.cos(ang)[None, :, None, :].astype(x.dtype)
    sin = jnp.sin(ang)[None, :, None, :].astype(x.dtype)
    x1 = x[..., :half]
    x2 = x[..., half:ROT_DIM]
    xp = x[..., ROT_DIM:]
    return jnp.concatenate([x1 * cos - x2 * sin, x2 * cos + x1 * sin, xp], axis=-1)


def diff_attention(h, wq, wk, wv, wo, lam_vecs, subln_g, lambda_init):
    B, S, _ = h.shape
    q = (h @ wq).reshape(B, S, 2 * DA_HEADS, DA_HEAD_DIM)
    k = (h @ wk).reshape(B, S, 2 * DA_HEADS, DA_HEAD_DIM)
    v = (h @ wv).reshape(B, S, DA_HEADS, DA_V_DIM)
    pos = jnp.arange(S, dtype=jnp.float32)
    q = rope_partial(q, pos) * (DA_HEAD_DIM ** -0.5)
    k = rope_partial(k, pos)
    lv = lam_vecs.astype(jnp.float32)
    lam = jnp.exp(jnp.sum(lv[0] * lv[1])) - jnp.exp(jnp.sum(lv[2] * lv[3])) + lambda_init
    outs = []
    for qb in range(S // Q_BLOCK):
        q0 = qb * Q_BLOCK
        k_end = q0 + Q_BLOCK
        s = jnp.einsum('bqhd,bkhd->bhqk', q[:, q0:k_end], k[:, :k_end]).astype(jnp.float32)
        q_chunk = (q0 + jnp.arange(Q_BLOCK)) // CHUNK
        k_chunk = jnp.arange(k_end) // CHUNK
        allowed = k_chunk[None, :] <= q_chunk[:, None]
        s = jnp.where(allowed[None, None], s, NEG_INF)
        p = jax.nn.softmax(s, axis=-1).reshape(B, DA_HEADS, 2, Q_BLOCK, k_end)
        pd = (p[:, :, 0] - lam * p[:, :, 1]).astype(v.dtype)
        outs.append(jnp.einsum('bhqk,bkhe->bqhe', pd, v[:, :k_end]))
    o = jnp.concatenate(outs, axis=1)
    o = rms_norm(o, subln_g, DA_SUBLN_EPS) * (1.0 - lambda_init)
    return o.reshape(B, S, DA_HEADS * DA_V_DIM) @ wo


def rwkv7_time_mix(h, mix, wr, wk, wv, wo, w0, w1, w2, a0, a1, a2, g1, g2,
                   k_k, k_a, r_k, lnx_g, lnx_b):
    B, S, D = h.shape
    f32 = jnp.float32
    xx = jnp.pad(h, ((0, 0), (1, 0), (0, 0)))[:, :-1] - h
    xr = h + xx * mix[0]
    xw = h + xx * mix[1]
    xk = h + xx * mix[2]
    xv = h + xx * mix[3]
    xa = h + xx * mix[4]
    xg = h + xx * mix[5]
    r = xr @ wr
    w = -jax.nn.softplus(-(w0 + jnp.tanh(xw @ w1) @ w2)) - 0.5
    k = xk @ wk
    v = xv @ wv
    a = jax.nn.sigmoid(a0 + (xa @ a1) @ a2)
    g = jax.nn.sigmoid(xg @ g1) @ g2

    def heads(t):
        return t.reshape(B, S, RW_HEADS, RW_HEAD).astype(f32)

    kk = heads(k * k_k)
    kk = kk / jnp.maximum(jnp.sqrt(jnp.sum(kk * kk, axis=-1, keepdims=True)), 1e-12)
    k = k * (1.0 + (a - 1.0) * k_a)
    r_h, k_h, v_h, a_h = heads(r), heads(k), heads(v), heads(a)
    decay = jnp.exp(-jnp.exp(heads(w)))

    def step(state, inp):
        rt, wt, kt, vt, kkt, at = inp
        sa = jnp.einsum('bhvk,bhk->bhv', state, -kkt)
        state = (state * wt[:, :, None, :]
                 + sa[..., None] * (kkt * at)[:, :, None, :]
                 + vt[..., None] * kt[:, :, None, :])
        yt = jnp.einsum('bhvk,bhk->bhv', state, rt)
        return state, yt

    xs = tuple(jnp.moveaxis(t, 1, 0) for t in (r_h, decay, k_h, v_h, kk, a_h))
    state0 = jnp.zeros((B, RW_HEADS, RW_HEAD, RW_HEAD), f32)
    _, y = lax.scan(step, state0, xs)
    y = jnp.moveaxis(y, 0, 1)
    mu = jnp.mean(y, axis=-1, keepdims=True)
    var = jnp.mean(jnp.square(y - mu), axis=-1, keepdims=True)
    y = ((y - mu) * lax.rsqrt(var + GN_EPS)).reshape(B, S, D)
    y = y * lnx_g.astype(f32) + lnx_b.astype(f32)
    bonus = jnp.sum(r_h * k_h * r_k.astype(f32), axis=-1, keepdims=True) * v_h
    out = (y + bonus.reshape(B, S, D)).astype(h.dtype) * g
    return out @ wo


def sqrelu_mlp(h, w1, w2):
    u = jax.nn.relu(h @ w1)
    return (u * u) @ w2


def setup_inputs(seed: int = 0) -> dict:
    key = jax.random.key(seed)
    ks = iter(jax.random.split(key, 48))
    f32 = jnp.float32

    def nrm(shape, scale):
        return jax.random.normal(next(ks), shape, f32) * scale

    def gain(shape):
        return 1.0 + 0.02 * jax.random.normal(next(ks), shape, f32)

    D = D_MODEL
    sd = D ** -0.5
    return {
        "x": nrm((BATCH, SEQ, D), 1.0),
        "g_pre_mix": gain((DEPTH, D)),
        "g_post_mix": gain((DEPTH, D)),
        "g_pre_ffn": gain((DEPTH, D)),
        "g_post_ffn": gain((DEPTH, D)),
        "ffn_w1": nrm((DEPTH, D, FFN), sd),
        "ffn_w2": nrm((DEPTH, FFN, D), FFN ** -0.5),
        "da_wq": nrm((N_ATTN, D, 2 * DA_HEADS * DA_HEAD_DIM), sd),
        "da_wk": nrm((N_ATTN, D, 2 * DA_HEADS * DA_HEAD_DIM), sd),
        "da_wv": nrm((N_ATTN, D, DA_HEADS * DA_V_DIM), sd),
        "da_wo": nrm((N_ATTN, DA_HEADS * DA_V_DIM, D), sd),
        "da_lambda": nrm((N_ATTN, 4, DA_HEAD_DIM), 0.1),
        "da_subln": gain((N_ATTN, DA_V_DIM)),
        "rw_mix": jax.random.uniform(next(ks), (N_RWKV, 6, D), f32),
        "rw_wr": nrm((N_RWKV, D, D), sd),
        "rw_wk": nrm((N_RWKV, D, D), sd),
        "rw_wv": nrm((N_RWKV, D, D), sd),
        "rw_wo": nrm((N_RWKV, D, D), sd),
        "rw_w0": jax.random.uniform(next(ks), (N_RWKV, D), f32, -6.0, -1.0),
        "rw_w1": nrm((N_RWKV, D, LORA_DECAY), sd),
        "rw_w2": nrm((N_RWKV, LORA_DECAY, D), 0.1 * LORA_DECAY ** -0.5),
        "rw_a0": nrm((N_RWKV, D), 0.1),
        "rw_a1": nrm((N_RWKV, D, LORA_A), sd),
        "rw_a2": nrm((N_RWKV, LORA_A, D), 0.1 * LORA_A ** -0.5),
        "rw_g1": nrm((N_RWKV, D, LORA_GATE), sd),
        "rw_g2": nrm((N_RWKV, LORA_GATE, D), LORA_GATE ** -0.5),
        "rw_kk": 0.85 + 0.05 * jax.random.normal(next(ks), (N_RWKV, D), f32),
        "rw_ka": gain((N_RWKV, D)),
        "rw_rk": nrm((N_RWKV, RW_HEADS, RW_HEAD), 0.1),
        "rw_lnx_g": gain((N_RWKV, D)),
        "rw_lnx_b": nrm((N_RWKV, D), 0.01),
    }


def reference(x, g_pre_mix, g_post_mix, g_pre_ffn, g_post_ffn, ffn_w1, ffn_w2,
              da_wq, da_wk, da_wv, da_wo, da_lambda, da_subln,
              rw_mix, rw_wr, rw_wk, rw_wv, rw_wo, rw_w0, rw_w1, rw_w2,
              rw_a0, rw_a1, rw_a2, rw_g1, rw_g2, rw_kk, rw_ka, rw_rk,
              rw_lnx_g, rw_lnx_b):
    for i in range(DEPTH):
        j = i // N_MIXERS
        hn = rms_norm(x, g_pre_mix[i])
        if i % N_MIXERS == 0:
            lambda_init = 0.8 - 0.6 * math.exp(-0.3 * i)
            m = diff_attention(hn, da_wq[j], da_wk[j], da_wv[j], da_wo[j],
                               da_lambda[j], da_subln[j], lambda_init)
        else:
            m = rwkv7_time_mix(hn, rw_mix[j], rw_wr[j], rw_wk[j], rw_wv[j], rw_wo[j],
                               rw_w0[j], rw_w1[j], rw_w2[j], rw_a0[j], rw_a1[j], rw_a2[j],
                               rw_g1[j], rw_g2[j], rw_kk[j], rw_ka[j], rw_rk[j],
                               rw_lnx_g[j], rw_lnx_b[j])
        x = x + rms_norm(m, g_post_mix[i])
        hn = rms_norm(x, g_pre_ffn[i])
        x = x + rms_norm(sqrelu_mlp(hn, ffn_w1[i], ffn_w2[i]), g_post_ffn[i])
    return x
```

```python
import functools
import math

import jax
import jax.numpy as jnp
from jax import lax
from jax.experimental import pallas as pl
from jax.experimental.pallas import tpu as pltpu

F32 = jnp.float32
BF16 = jnp.bfloat16

CHUNK = 64
DA_HEADS = 8
DA_HEAD_DIM = 128
DA_V_DIM = 2 * DA_HEAD_DIM
ROT_DIM = DA_HEAD_DIM // 4
ROPE_THETA = 500000.0
DA_SUBLN_EPS = 1e-5
RW_HEAD = 64
GN_EPS = 64e-5
EPS = 1e-6
NEG_INF = -1e30
N_MIXERS = 2

VMEM_LIMIT_BYTES = 52 * 1024 * 1024
LANES = 128
WKV_CHUNK = 64


def _params(*sem):
    return pltpu.CompilerParams(dimension_semantics=sem, vmem_limit_bytes=VMEM_LIMIT_BYTES)


def _rms(x, eps):
    return x * lax.rsqrt(jnp.mean(x * x, axis=-1, keepdims=True) + eps)


def _dot(a, b):
    return jnp.dot(a, b, preferred_element_type=F32)


def _dot_nt(a, b):
    return lax.dot_general(a, b, (((1,), (1,)), ((), ())), preferred_element_type=F32)


def _dot_tn(a, b):
    return lax.dot_general(a, b, (((0,), (0,)), ((), ())), preferred_element_type=F32)


def _qkv_kernel(x_ref, g_ref, w_ref, tab_ref, o_ref, a_scr, *, n_rope_tiles):
    j = pl.program_id(1)

    @pl.when(j == 0)
    def _():
        a_scr[...] = (_rms(x_ref[...], EPS) * g_ref[...]).astype(BF16)

    acc = _dot(a_scr[...], w_ref[...])
    tn = acc.shape[1]

    @pl.when(j < n_rope_tiles)
    def _():
        c, s1, s2 = tab_ref[0], tab_ref[1], tab_ref[2]
        half = ROT_DIM // 2
        for h in range(tn // DA_HEAD_DIM):
            blk = acc[:, h * DA_HEAD_DIM:(h + 1) * DA_HEAD_DIM]
            up = pltpu.roll(blk, DA_HEAD_DIM - half, axis=1)
            dn = pltpu.roll(blk, half, axis=1)
            o_ref[:, h * DA_HEAD_DIM:(h + 1) * DA_HEAD_DIM] = (blk * c + up * s1 + dn * s2).astype(o_ref.dtype)

    @pl.when(j >= n_rope_tiles)
    def _():
        o_ref[...] = acc.astype(o_ref.dtype)


def _qkv_proj(x, g, w, tabs, seq, *, tm=1024, tn=512):
    T, D = x.shape
    N = w.shape[1]
    n_q_tiles = (N // 3) // tn
    tiles_per_seq = seq // tm
    return pl.pallas_call(
        functools.partial(_qkv_kernel, n_rope_tiles=2 * n_q_tiles),
        grid=(T // tm, N // tn),
        in_specs=[
            pl.BlockSpec((tm, D), lambda i, j: (i, 0)),
            pl.BlockSpec((1, D), lambda i, j: (0, 0)),
            pl.BlockSpec((D, tn), lambda i, j: (0, j)),
            pl.BlockSpec((None, 3, tm, DA_HEAD_DIM),
                         lambda i, j: (jnp.minimum(j // n_q_tiles, 1), 0, i % tiles_per_seq, 0)),
        ],
        out_specs=pl.BlockSpec((tm, tn), lambda i, j: (i, j)),
        out_shape=jax.ShapeDtypeStruct((T, N), BF16),
        scratch_shapes=[pltpu.VMEM((tm, D), BF16)],
        compiler_params=_params("parallel", "arbitrary"),
        name="qkv_proj",
    )(x, g.reshape(1, D), w, tabs)


def _rope_tables(seq):
    half = ROT_DIM // 2
    inv_freq = ROPE_THETA ** (-jnp.arange(half, dtype=F32) * (2.0 / ROT_DIM))
    ang = jnp.arange(seq, dtype=F32)[:, None] * inv_freq[None, :]
    cos, sin = jnp.cos(ang), jnp.sin(ang)
    rest = DA_HEAD_DIM - ROT_DIM
    c = jnp.concatenate([cos, cos, jnp.ones((seq, rest), F32)], axis=1)
    s1 = jnp.concatenate([-sin, jnp.zeros((seq, DA_HEAD_DIM - half), F32)], axis=1)
    s2 = jnp.concatenate([jnp.zeros((seq, half), F32), sin, jnp.zeros((seq, rest), F32)], axis=1)
    k_tab = jnp.stack([c, s1, s2])
    return jnp.stack([k_tab * (DA_HEAD_DIM ** -0.5), k_tab])


def _attn_kernel(lam_ref, q_ref, k_ref, v_ref, g_ref, o_ref, *, blk, lambda_init):
    qi = pl.program_id(2)
    q = q_ref[...]
    qs = (q[:, :DA_HEAD_DIM], q[:, DA_HEAD_DIM:])
    row = lax.broadcasted_iota(jnp.int32, (blk, blk), 0)
    col = lax.broadcasted_iota(jnp.int32, (blk, blk), 1)
    allowed = (col // CHUNK) <= (row // CHUNK)

    def step(kb, carry, masked):
        start = pl.multiple_of(kb * blk, blk)
        k = k_ref[pl.ds(start, blk), :]
        v = v_ref[pl.ds(start, blk), :]
        out = []
        for m in range(2):
            mx, l, acc = carry[m]
            s = _dot_nt(qs[m], k[:, m * DA_HEAD_DIM:(m + 1) * DA_HEAD_DIM])
            if masked:
                s = jnp.where(allowed, s, NEG_INF)
            mx_new = jnp.maximum(mx, jnp.max(s, axis=-1, keepdims=True))
            p = jnp.exp(s - mx_new)
            alpha = jnp.exp(mx - mx_new)
            l = alpha * l + jnp.sum(p, axis=-1, keepdims=True)
            acc = alpha * acc + _dot(p.astype(BF16), v)
            out.append((mx_new, l, acc))
        return tuple(out)

    init = tuple((jnp.full((blk, 1), NEG_INF, F32), jnp.zeros((blk, 1), F32),
                  jnp.zeros((blk, DA_V_DIM), F32)) for _ in range(2))
    carry = lax.fori_loop(0, qi, lambda kb, c: step(kb, c, False), init)
    (_, l1, a1), (_, l2, a2) = step(qi, carry, True)

    lv = lam_ref[...]
    lam = (jnp.exp(jnp.sum(lv[0:1] * lv[1:2], axis=-1, keepdims=True))
           - jnp.exp(jnp.sum(lv[2:3] * lv[3:4], axis=-1, keepdims=True)) + lambda_init)
    o = a1 * (1.0 / l1) - lam * (a2 * (1.0 / l2))
    o = _rms(o, DA_SUBLN_EPS) * g_ref[...] * (1.0 - lambda_init)
    o_ref[...] = o.astype(o_ref.dtype)


def _diff_attention(qkv, lam_vecs, subln_g, lambda_init, batch, seq, *, blk=256):
    T = qkv.shape[0]
    nq = seq // blk
    return pl.pallas_call(
        functools.partial(_attn_kernel, blk=blk, lambda_init=lambda_init),
        grid=(batch, DA_HEADS, nq),
        in_specs=[
            pl.BlockSpec((4, DA_HEAD_DIM), lambda b, h, i: (0, 0)),
            pl.BlockSpec((blk, DA_V_DIM), lambda b, h, i: (b * nq + i, h)),
            pl.BlockSpec((seq, DA_V_DIM), lambda b, h, i: (b, DA_HEADS + h)),
            pl.BlockSpec((seq, DA_V_DIM), lambda b, h, i: (b, 2 * DA_HEADS + h)),
            pl.BlockSpec((1, DA_V_DIM), lambda b, h, i: (0, 0)),
        ],
        out_specs=pl.BlockSpec((blk, DA_V_DIM), lambda b, h, i: (b * nq + i, h)),
        out_shape=jax.ShapeDtypeStruct((T, DA_HEADS * DA_V_DIM), BF16),
        compiler_params=_params("parallel", "parallel", "arbitrary"),
        name="diff_attn",
    )(lam_vecs, qkv, qkv, qkv, subln_g.reshape(1, DA_V_DIM))


def _proj_norm_res_kernel(a_ref, w_ref, g_ref, x_ref, o_ref):
    m = _dot(a_ref[...], w_ref[...])
    o_ref[...] = x_ref[...] + _rms(m, EPS) * g_ref[...]


def _proj_norm_res(a, w, g, resid, *, tm=512):
    T, K = a.shape
    N = w.shape[1]
    return pl.pallas_call(
        _proj_norm_res_kernel,
        grid=(T // tm,),
        in_specs=[
            pl.BlockSpec((tm, K), lambda i: (i, 0)),
            pl.BlockSpec((K, N), lambda i: (0, 0)),
            pl.BlockSpec((1, N), lambda i: (0, 0)),
            pl.BlockSpec((tm, N), lambda i: (i, 0)),
        ],
        out_specs=pl.BlockSpec((tm, N), lambda i: (i, 0)),
        out_shape=jax.ShapeDtypeStruct((T, N), F32),
        compiler_params=_params("parallel"),
        name="proj_norm_res",
    )(a, w, g.reshape(1, N), resid)


def _ffn_kernel(x_ref, g1_ref, w1_ref, w2_ref, g2_ref, o_ref, h_scr, acc_scr):
    f = pl.program_id(1)

    @pl.when(f == 0)
    def _():
        h_scr[...] = (_rms(x_ref[...], EPS) * g1_ref[...]).astype(BF16)
        acc_scr[...] = jnp.zeros_like(acc_scr)

    u = jnp.maximum(_dot(h_scr[...], w1_ref[...]), 0.0)
    acc_scr[...] += _dot((u * u).astype(BF16), w2_ref[...])

    @pl.when(f == pl.num_programs(1) - 1)
    def _():
        o_ref[...] = x_ref[...] + _rms(acc_scr[...], EPS) * g2_ref[...]


def _ffn(x, g1, w1, w2, g2, *, tm=512, tf=512):
    T, D = x.shape
    F = w1.shape[1]
    return pl.pallas_call(
        _ffn_kernel,
        grid=(T // tm, F // tf),
        in_specs=[
            pl.BlockSpec((tm, D), lambda i, f: (i, 0)),
            pl.BlockSpec((1, D), lambda i, f: (0, 0)),
            pl.BlockSpec((D, tf), lambda i, f: (0, f)),
            pl.BlockSpec((tf, D), lambda i, f: (f, 0)),
            pl.BlockSpec((1, D), lambda i, f: (0, 0)),
        ],
        out_specs=pl.BlockSpec((tm, D), lambda i, f: (i, 0)),
        out_shape=jax.ShapeDtypeStruct((T, D), F32),
        scratch_shapes=[pltpu.VMEM((tm, D), BF16), pltpu.VMEM((tm, D), F32)],
        compiler_params=_params("parallel", "arbitrary"),
        name="ffn",
    )(x, g1.reshape(1, D), w1, w2, g2.reshape(1, D))


def _mix_kernel(x_ref, xp_ref, g_ref, mix_ref, o_ref, *, tiles_per_seq):
    i = pl.program_id(0)
    g = g_ref[...]
    hn = _rms(x_ref[...], EPS) * g
    sub = xp_ref.shape[0]
    hp = _rms(xp_ref[sub - 1:sub, :], EPS) * g
    hp = jnp.where(i % tiles_per_seq == 0, jnp.zeros_like(hp), hp)
    row = lax.broadcasted_iota(jnp.int32, (hn.shape[0], 1), 0)
    prev = jnp.where(row == 0, hp, pltpu.roll(hn, 1, axis=0))
    xx = prev - hn
    for c in range(o_ref.shape[0]):
        o_ref[c] = (hn + xx * mix_ref[c:c + 1, :]).astype(o_ref.dtype)


def _rwkv_mix(x, g, mix, seq, *, tm=256, sub=8):
    T, D = x.shape
    n_mix = mix.shape[0]
    per = tm // sub
    return pl.pallas_call(
        functools.partial(_mix_kernel, tiles_per_seq=seq // tm),
        grid=(T // tm,),
        in_specs=[
            pl.BlockSpec((tm, D), lambda i: (i, 0)),
            pl.BlockSpec((sub, D), lambda i: (jnp.maximum(i * per - 1, 0), 0)),
            pl.BlockSpec((1, D), lambda i: (0, 0)),
            pl.BlockSpec((n_mix, D), lambda i: (0, 0)),
        ],
        out_specs=pl.BlockSpec((n_mix, tm, D), lambda i: (0, i, 0)),
        out_shape=jax.ShapeDtypeStruct((n_mix, T, D), BF16),
        compiler_params=_params("parallel"),
        name="rwkv_mix",
    )(x, x, g.reshape(1, D), mix)


def _bmm_kernel(a_ref, w_ref, o_ref):
    o_ref[...] = _dot(a_ref[...], w_ref[...])


def _bmm(a, w, *, tm=1024, tn=512):
    P, K, N = w.shape
    T = a.shape[1]
    return pl.pallas_call(
        _bmm_kernel,
        grid=(P, T // tm, N // tn),
        in_specs=[
            pl.BlockSpec((None, tm, K), lambda p, i, j: (p, i, 0)),
            pl.BlockSpec((None, K, tn), lambda p, i, j: (p, 0, j)),
        ],
        out_specs=pl.BlockSpec((None, tm, tn), lambda p, i, j: (p, i, j)),
        out_shape=jax.ShapeDtypeStruct((P, T, N), F32),
        compiler_params=_params("parallel", "parallel", "arbitrary"),
        name="rkv_proj",
    )(a, w)


def _lora_kernel(a_ref, w1_ref, w2_ref, *rest, kind):
    o_ref = rest[-1]
    t = _dot(a_ref[...], w1_ref[...])
    if kind == "decay":
        t = jnp.tanh(t)
    elif kind == "gate":
        t = jax.nn.sigmoid(t)
    z = _dot(t.astype(BF16), w2_ref[...])
    if kind != "gate":
        z = z + rest[0][...]
    if kind == "decay":
        z = -math.exp(-0.5) * jax.nn.sigmoid(z)
    elif kind == "rate":
        z = jax.nn.sigmoid(z)
    o_ref[...] = z


def _lora(mixed, slot, w1, w2, bias, kind, *, tm=512):
    _, T, D = mixed.shape
    R = w1.shape[1]
    in_specs = [
        pl.BlockSpec((None, tm, D), lambda i: (slot, i, 0)),
        pl.BlockSpec((D, R), lambda i: (0, 0)),
        pl.BlockSpec((R, D), lambda i: (0, 0)),
    ]
    args = [mixed, w1, w2]
    if bias is not None:
        in_specs.append(pl.BlockSpec((1, D), lambda i: (0, 0)))
        args.append(bias.reshape(1, D))
    return pl.pallas_call(
        functools.partial(_lora_kernel, kind=kind),
        grid=(T // tm,),
        in_specs=in_specs,
        out_specs=pl.BlockSpec((tm, D), lambda i: (i, 0)),
        out_shape=jax.ShapeDtypeStruct((T, D), F32),
        compiler_params=_params("parallel"),
        name="lora_" + kind,
    )(*args)


def _cumsum_rows(tri, x):
    hi = x.astype(BF16)
    r1 = x - hi.astype(F32)
    mid = r1.astype(BF16)
    lo = (r1 - mid.astype(F32)).astype(BF16)
    return _dot(tri, hi) + _dot(tri, mid) + _dot(tri, lo)


def _wkv_kernel(r_ref, k_ref, v_ref, lw_ref, a_ref, g_ref, kkp_ref, kap_ref, rkp_ref, lng_ref, lnb_ref,
                o_ref, s_scr, *, n_chunks):
    C = WKV_CHUNK
    N = RW_HEAD
    heads = s_scr.shape[0]

    @pl.when(pl.program_id(2) == 0)
    def _():
        s_scr[...] = jnp.zeros_like(s_scr)

    row = lax.broadcasted_iota(jnp.int32, (C, C), 0)
    col = lax.broadcasted_iota(jnp.int32, (C, C), 1)
    strict = row > col
    incl = row >= col
    eye = (row == col).astype(F32)
    tri = incl.astype(BF16)
    base_mask = (row // 2) == (col // 2)
    level_masks = []
    s = 2
    while s < C:
        level_masks.append(((row // (2 * s)) == (col // (2 * s))) & ((row // s) != (col // s)))
        s *= 2

    def chunk(c, _):
        t0 = pl.multiple_of(c * C, C)
        sl = pl.ds(t0, C)
        r_all, k_all, v_all = r_ref[sl, :], k_ref[sl, :], v_ref[sl, :]
        a_all = a_ref[sl, :]
        L_all = _cumsum_rows(tri, lw_ref[sl, :])
        Lp_all = L_all - lw_ref[sl, :]
        for h in range(heads):
            hs = slice(h * N, (h + 1) * N)
            r, kraw, v, a = r_all[:, hs], k_all[:, hs], v_all[:, hs], a_all[:, hs]
            L, Lp = L_all[:, hs], Lp_all[:, hs]
            kk = kraw * kkp_ref[:, hs]
            kk = kk / jnp.maximum(jnp.sqrt(jnp.sum(kk * kk, axis=-1, keepdims=True)), 1e-12)
            k = kraw * (1.0 + (a - 1.0) * kap_ref[:, hs])
            Lc = L[C - 1:C, :]
            e_nl = jnp.exp(-L)
            e_end = jnp.exp(Lc - L)
            b = kk * a
            at = -kk * jnp.exp(Lp)
            rt = r * jnp.exp(L)
            lhs = jnp.concatenate([at, rt], axis=0).astype(BF16)
            rhs = jnp.concatenate([b * e_nl, k * e_nl], axis=0).astype(BF16)
            G = _dot_nt(lhs, rhs)
            A_ab = jnp.where(strict, G[:C, :C], 0.0)
            A_ak = jnp.where(strict, G[:C, C:], 0.0)
            P_rb = jnp.where(incl, G[C:, :C], 0.0)
            P_rk = jnp.where(incl, G[C:, C:], 0.0)
            X = eye + jnp.where(base_mask, A_ab, 0.0)
            for off in level_masks:
                Xb = X.astype(BF16)
                X = X + _dot(Xb, _dot(jnp.where(off, A_ab, 0.0).astype(BF16), Xb).astype(BF16))
            S0 = s_scr[h]
            SA = _dot_nt(lhs, S0.astype(BF16))
            vb = v.astype(BF16)
            W = SA[:C] + _dot(A_ak.astype(BF16), vb)
            U = _dot(X.astype(BF16), W.astype(BF16))
            ub = U.astype(BF16)
            Y = SA[C:] + _dot(P_rb.astype(BF16), ub) + _dot(P_rk.astype(BF16), vb)
            s_scr[h] = (S0 * jnp.exp(Lc) + _dot_tn(ub, (b * e_end).astype(BF16))
                        + _dot_tn(vb, (k * e_end).astype(BF16)))
            mu = jnp.mean(Y, axis=-1, keepdims=True)
            yc = Y - mu
            var = jnp.mean(yc * yc, axis=-1, keepdims=True)
            yn = yc * lax.rsqrt(var + GN_EPS) * lng_ref[:, hs] + lnb_ref[:, hs]
            bonus = jnp.sum(r * k * rkp_ref[:, hs], axis=-1, keepdims=True) * v
            o_ref[sl, hs] = ((yn + bonus) * g_ref[sl, hs]).astype(o_ref.dtype)
        return 0

    lax.fori_loop(0, n_chunks, chunk, 0)


def _wkv(rkv, lw, a, g, kkp, kap, rkp, lng, lnb, batch, seq, *, hw=128, tb=512):
    _, T, D = rkv.shape
    nt = seq // tb
    tok = lambda b, j, t: (b * nt + t, j)
    par = pl.BlockSpec((1, hw), lambda b, j, t: (0, j))
    return pl.pallas_call(
        functools.partial(_wkv_kernel, n_chunks=tb // WKV_CHUNK),
        grid=(batch, D // hw, nt),
        in_specs=[
            pl.BlockSpec((None, tb, hw), lambda b, j, t: (0, b * nt + t, j)),
            pl.BlockSpec((None, tb, hw), lambda b, j, t: (1, b * nt + t, j)),
            pl.BlockSpec((None, tb, hw), lambda b, j, t: (2, b * nt + t, j)),
            pl.BlockSpec((tb, hw), tok),
            pl.BlockSpec((tb, hw), tok),
            pl.BlockSpec((tb, hw), tok),
            par, par, par, par, par,
        ],
        out_specs=pl.BlockSpec((tb, hw), tok),
        out_shape=jax.ShapeDtypeStruct((T, D), BF16),
        scratch_shapes=[pltpu.VMEM((hw // RW_HEAD, RW_HEAD, RW_HEAD), F32)],
        compiler_params=_params("parallel", "parallel", "arbitrary"),
        name="wkv7",
    )(rkv, rkv, rkv, lw, a, g, kkp.reshape(1, D), kap.reshape(1, D), rkp.reshape(1, D),
      lng.reshape(1, D), lnb.reshape(1, D))


def _pad_rank(w1, w2):
    pad = (-w1.shape[1]) % LANES
    return jnp.pad(w1, ((0, 0), (0, pad))), jnp.pad(w2, ((0, pad), (0, 0)))


def kernel(x, g_pre_mix, g_post_mix, g_pre_ffn, g_post_ffn, ffn_w1, ffn_w2, da_wq, da_wk, da_wv, da_wo, da_lambda, da_subln, rw_mix, rw_wr, rw_wk, rw_wv, rw_wo, rw_w0, rw_w1, rw_w2, rw_a0, rw_a1, rw_a2, rw_g1, rw_g2, rw_kk, rw_ka, rw_rk, rw_lnx_g, rw_lnx_b):
    B, S, D = x.shape
    depth = g_pre_mix.shape[0]
    h = x.reshape(B * S, D)
    tabs = _rope_tables(S)
    for i in range(depth):
        j = i // N_MIXERS
        if i % N_MIXERS == 0:
            lambda_init = 0.8 - 0.6 * math.exp(-0.3 * i)
            wqkv = jnp.concatenate([da_wq[j], da_wk[j], da_wv[j]], axis=1).astype(BF16)
            qkv = _qkv_proj(h, g_pre_mix[i], wqkv, tabs, S)
            att = _diff_attention(qkv, da_lambda[j], da_subln[j], lambda_init, B, S)
            h = _proj_norm_res(att, da_wo[j].astype(BF16), g_post_mix[i], h)
        else:
            mix = rw_mix[j][jnp.array([0, 2, 3, 1, 4, 5])]
            mixed = _rwkv_mix(h, g_pre_mix[i], mix, S)
            rkv = _bmm(mixed, jnp.stack([rw_wr[j], rw_wk[j], rw_wv[j]]).astype(BF16))
            w1, w2 = _pad_rank(rw_w1[j], rw_w2[j])
            a1, a2 = _pad_rank(rw_a1[j], rw_a2[j])
            g1, g2 = _pad_rank(rw_g1[j], rw_g2[j])
            lw = _lora(mixed, 3, w1.astype(BF16), w2.astype(BF16), rw_w0[j], "decay")
            rate = _lora(mixed, 4, a1.astype(BF16), a2.astype(BF16), rw_a0[j], "rate")
            gate = _lora(mixed, 5, g1.astype(BF16), g2.astype(BF16), None, "gate")
            y = _wkv(rkv, lw, rate, gate, rw_kk[j], rw_ka[j], rw_rk[j].reshape(-1), rw_lnx_g[j], rw_lnx_b[j], B, S)
            h = _proj_norm_res(y, rw_wo[j].astype(BF16), g_post_mix[i], h)
        h = _ffn(h, g_pre_ffn[i], ffn_w1[i].astype(BF16), ffn_w2[i].astype(BF16), g_post_ffn[i])
    return h.reshape(B, S, D)
```

```python
import functools
import math

import jax
import jax.numpy as jnp
from jax import lax
from jax.experimental import pallas as pl
from jax.experimental.pallas import tpu as pltpu

F32 = jnp.float32
BF16 = jnp.bfloat16

CHUNK = 64
DA_HEADS = 8
DA_HEAD_DIM = 128
DA_V_DIM = 2 * DA_HEAD_DIM
ROT_DIM = DA_HEAD_DIM // 4
ROPE_THETA = 500000.0
DA_SUBLN_EPS = 1e-5
RW_HEAD = 64
GN_EPS = 64e-5
EPS = 1e-6
NEG_INF = -1e30
N_MIXERS = 2

VMEM_LIMIT_BYTES = 52 * 1024 * 1024
LANES = 128
WKV_CHUNK = 64


def _params(*sem):
    return pltpu.CompilerParams(dimension_semantics=sem, vmem_limit_bytes=VMEM_LIMIT_BYTES)


def _rms(x, eps):
    return x * lax.rsqrt(jnp.mean(x * x, axis=-1, keepdims=True) + eps)


def _dot(a, b):
    return jnp.dot(a, b, preferred_element_type=F32)


def _dot_nt(a, b):
    return lax.dot_general(a, b, (((1,), (1,)), ((), ())), preferred_element_type=F32)


def _dot_tn(a, b):
    return lax.dot_general(a, b, (((0,), (0,)), ((), ())), preferred_element_type=F32)


def _qkv_kernel(x_ref, g_ref, w_ref, tab_ref, o_ref, a_scr, *, n_rope_tiles):
    j = pl.program_id(1)

    @pl.when(j == 0)
    def _():
        a_scr[...] = (_rms(x_ref[...], EPS) * g_ref[...]).astype(BF16)

    acc = _dot(a_scr[...], w_ref[...])
    tn = acc.shape[1]

    @pl.when(j < n_rope_tiles)
    def _():
        c, s1, s2 = tab_ref[0], tab_ref[1], tab_ref[2]
        half = ROT_DIM // 2
        for h in range(tn // DA_HEAD_DIM):
            blk = acc[:, h * DA_HEAD_DIM:(h + 1) * DA_HEAD_DIM]
            up = pltpu.roll(blk, DA_HEAD_DIM - half, axis=1)
            dn = pltpu.roll(blk, half, axis=1)
            o_ref[:, h * DA_HEAD_DIM:(h + 1) * DA_HEAD_DIM] = (blk * c + up * s1 + dn * s2).astype(o_ref.dtype)

    @pl.when(j >= n_rope_tiles)
    def _():
        o_ref[...] = acc.astype(o_ref.dtype)


def _qkv_proj(x, g, w, tabs, seq, *, tm=1024, tn=512):
    T, D = x.shape
    N = w.shape[1]
    n_q_tiles = (N // 3) // tn
    tiles_per_seq = seq // tm
    return pl.pallas_call(
        functools.partial(_qkv_kernel, n_rope_tiles=2 * n_q_tiles),
        grid=(T // tm, N // tn),
        in_specs=[
            pl.BlockSpec((tm, D), lambda i, j: (i, 0)),
            pl.BlockSpec((1, D), lambda i, j: (0, 0)),
            pl.BlockSpec((D, tn), lambda i, j: (0, j)),
            pl.BlockSpec((None, 3, tm, DA_HEAD_DIM),
                         lambda i, j: (jnp.minimum(j // n_q_tiles, 1), 0, i % tiles_per_seq, 0)),
        ],
        out_specs=pl.BlockSpec((tm, tn), lambda i, j: (i, j)),
        out_shape=jax.ShapeDtypeStruct((T, N), BF16),
        scratch_shapes=[pltpu.VMEM((tm, D), BF16)],
        compiler_params=_params("parallel", "arbitrary"),
        name="qkv_proj",
    )(x, g.reshape(1, D), w, tabs)


def _rope_tables(seq):
    half = ROT_DIM // 2
    inv_freq = ROPE_THETA ** (-jnp.arange(half, dtype=F32) * (2.0 / ROT_DIM))
    ang = jnp.arange(seq, dtype=F32)[:, None] * inv_freq[None, :]
    cos, sin = jnp.cos(ang), jnp.sin(ang)
    rest = DA_HEAD_DIM - ROT_DIM
    c = jnp.concatenate([cos, cos, jnp.ones((seq, rest), F32)], axis=1)
    s1 = jnp.concatenate([-sin, jnp.zeros((seq, DA_HEAD_DIM - half), F32)], axis=1)
    s2 = jnp.concatenate([jnp.zeros((seq, half), F32), sin, jnp.zeros((seq, rest), F32)], axis=1)
    k_tab = jnp.stack([c, s1, s2])
    return jnp.stack([k_tab * (DA_HEAD_DIM ** -0.5), k_tab])


def _attn_kernel(lam_ref, q_ref, k_ref, v_ref, g_ref, o_ref, *, blk, lambda_init):
    qi = pl.program_id(2)
    q = q_ref[...]
    qs = (q[:, :DA_HEAD_DIM], q[:, DA_HEAD_DIM:])
    row = lax.broadcasted_iota(jnp.int32, (blk, blk), 0)
    col = lax.broadcasted_iota(jnp.int32, (blk, blk), 1)
    allowed = (col // CHUNK) <= (row // CHUNK)

    def step(kb, carry, masked):
        start = pl.multiple_of(kb * blk, blk)
        k = k_ref[pl.ds(start, blk), :]
        v = v_ref[pl.ds(start, blk), :]
        out = []
        for m in range(2):
            mx, l, acc = carry[m]
            s = _dot_nt(qs[m], k[:, m * DA_HEAD_DIM:(m + 1) * DA_HEAD_DIM])
            if masked:
                s = jnp.where(allowed, s, NEG_INF)
            mx_new = jnp.maximum(mx, jnp.max(s, axis=-1, keepdims=True))
            p = jnp.exp(s - mx_new)
            alpha = jnp.exp(mx - mx_new)
            l = alpha * l + jnp.sum(p, axis=-1, keepdims=True)
            acc = alpha * acc + _dot(p.astype(BF16), v)
            out.append((mx_new, l, acc))
        return tuple(out)

    init = tuple((jnp.full((blk, 1), NEG_INF, F32), jnp.zeros((blk, 1), F32),
                  jnp.zeros((blk, DA_V_DIM), F32)) for _ in range(2))
    carry = lax.fori_loop(0, qi, lambda kb, c: step(kb, c, False), init)
    (_, l1, a1), (_, l2, a2) = step(qi, carry, True)

    lv = lam_ref[...]
    lam = (jnp.exp(jnp.sum(lv[0:1] * lv[1:2], axis=-1, keepdims=True))
           - jnp.exp(jnp.sum(lv[2:3] * lv[3:4], axis=-1, keepdims=True)) + lambda_init)
    o = a1 * (1.0 / l1) - lam * (a2 * (1.0 / l2))
    o = _rms(o, DA_SUBLN_EPS) * g_ref[...] * (1.0 - lambda_init)
    o_ref[...] = o.astype(o_ref.dtype)


def _diff_attention(qkv, lam_vecs, subln_g, lambda_init, batch, seq, *, blk=256):
    T = qkv.shape[0]
    nq = seq // blk
    return pl.pallas_call(
        functools.partial(_attn_kernel, blk=blk, lambda_init=lambda_init),
        grid=(batch, DA_HEADS, nq),
        in_specs=[
            pl.BlockSpec((4, DA_HEAD_DIM), lambda b, h, i: (0, 0)),
            pl.BlockSpec((blk, DA_V_DIM), lambda b, h, i: (b * nq + i, h)),
            pl.BlockSpec((seq, DA_V_DIM), lambda b, h, i: (b, DA_HEADS + h)),
            pl.BlockSpec((seq, DA_V_DIM), lambda b, h, i: (b, 2 * DA_HEADS + h)),
            pl.BlockSpec((1, DA_V_DIM), lambda b, h, i: (0, 0)),
        ],
        out_specs=pl.BlockSpec((blk, DA_V_DIM), lambda b, h, i: (b * nq + i, h)),
        out_shape=jax.ShapeDtypeStruct((T, DA_HEADS * DA_V_DIM), BF16),
        compiler_params=_params("parallel", "parallel", "arbitrary"),
        name="diff_attn",
    )(lam_vecs, qkv, qkv, qkv, subln_g.reshape(1, DA_V_DIM))


def _proj_norm_res_kernel(a_ref, w_ref, g_ref, x_ref, o_ref):
    m = _dot(a_ref[...], w_ref[...])
    o_ref[...] = x_ref[...] + _rms(m, EPS) * g_ref[...]


def _proj_norm_res(a, w, g, resid, *, tm=512):
    T, K = a.shape
    N = w.shape[1]
    return pl.pallas_call(
        _proj_norm_res_kernel,
        grid=(T // tm,),
        in_specs=[
            pl.BlockSpec((tm, K), lambda i: (i, 0)),
            pl.BlockSpec((K, N), lambda i: (0, 0)),
            pl.BlockSpec((1, N), lambda i: (0, 0)),
            pl.BlockSpec((tm, N), lambda i: (i, 0)),
        ],
        out_specs=pl.BlockSpec((tm, N), lambda i: (i, 0)),
        out_shape=jax.ShapeDtypeStruct((T, N), F32),
        compiler_params=_params("parallel"),
        name="proj_norm_res",
    )(a, w, g.reshape(1, N), resid)


def _ffn_kernel(x_ref, g1_ref, w1_ref, w2_ref, g2_ref, o_ref, h_scr, acc_scr):
    f = pl.program_id(1)

    @pl.when(f == 0)
    def _():
        h_scr[...] = (_rms(x_ref[...], EPS) * g1_ref[...]).astype(BF16)
        acc_scr[...] = jnp.zeros_like(acc_scr)

    u = jnp.maximum(_dot(h_scr[...], w1_ref[...]), 0.0)
    acc_scr[...] += _dot((u * u).astype(BF16), w2_ref[...])

    @pl.when(f == pl.num_programs(1) - 1)
    def _():
        o_ref[...] = x_ref[...] + _rms(acc_scr[...], EPS) * g2_ref[...]


def _ffn(x, g1, w1, w2, g2, *, tm=512, tf=512):
    T, D = x.shape
    F = w1.shape[1]
    return pl.pallas_call(
        _ffn_kernel,
        grid=(T // tm, F // tf),
        in_specs=[
            pl.BlockSpec((tm, D), lambda i, f: (i, 0)),
            pl.BlockSpec((1, D), lambda i, f: (0, 0)),
            pl.BlockSpec((D, tf), lambda i, f: (0, f)),
            pl.BlockSpec((tf, D), lambda i, f: (f, 0)),
            pl.BlockSpec((1, D), lambda i, f: (0, 0)),
        ],
        out_specs=pl.BlockSpec((tm, D), lambda i, f: (i, 0)),
        out_shape=jax.ShapeDtypeStruct((T, D), F32),
        scratch_shapes=[pltpu.VMEM((tm, D), BF16), pltpu.VMEM((tm, D), F32)],
        compiler_params=_params("parallel", "arbitrary"),
        name="ffn",
    )(x, g1.reshape(1, D), w1, w2, g2.reshape(1, D))


def _mix_kernel(x_ref, xp_ref, g_ref, mix_ref, o_ref, *, tiles_per_seq):
    i = pl.program_id(0)
    g = g_ref[...]
    hn = _rms(x_ref[...], EPS) * g
    sub = xp_ref.shape[0]
    hp = _rms(xp_ref[sub - 1:sub, :], EPS) * g
    hp = jnp.where(i % tiles_per_seq == 0, jnp.zeros_like(hp), hp)
    row = lax.broadcasted_iota(jnp.int32, (hn.shape[0], 1), 0)
    prev = jnp.where(row == 0, hp, pltpu.roll(hn, 1, axis=0))
    xx = prev - hn
    for c in range(o_ref.shape[0]):
        o_ref[c] = (hn + xx * mix_ref[c:c + 1, :]).astype(o_ref.dtype)


def _rwkv_mix(x, g, mix, seq, *, tm=256, sub=8):
    T, D = x.shape
    n_mix = mix.shape[0]
    per = tm // sub
    return pl.pallas_call(
        functools.partial(_mix_kernel, tiles_per_seq=seq // tm),
        grid=(T // tm,),
        in_specs=[
            pl.BlockSpec((tm, D), lambda i: (i, 0)),
            pl.BlockSpec((sub, D), lambda i: (jnp.maximum(i * per - 1, 0), 0)),
            pl.BlockSpec((1, D), lambda i: (0, 0)),
            pl.BlockSpec((n_mix, D), lambda i: (0, 0)),
        ],
        out_specs=pl.BlockSpec((n_mix, tm, D), lambda i: (0, i, 0)),
        out_shape=jax.ShapeDtypeStruct((n_mix, T, D), BF16),
        compiler_params=_params("parallel"),
        name="rwkv_mix",
    )(x, x, g.reshape(1, D), mix)


def _bmm_kernel(a_ref, w_ref, o_ref):
    o_ref[...] = _dot(a_ref[...], w_ref[...])


def _bmm(a, w, *, tm=1024, tn=512):
    P, K, N = w.shape
    T = a.shape[1]
    return pl.pallas_call(
        _bmm_kernel,
        grid=(P, T // tm, N // tn),
        in_specs=[
            pl.BlockSpec((None, tm, K), lambda p, i, j: (p, i, 0)),
            pl.BlockSpec((None, K, tn), lambda p, i, j: (p, 0, j)),
        ],
        out_specs=pl.BlockSpec((None, tm, tn), lambda p, i, j: (p, i, j)),
        out_shape=jax.ShapeDtypeStruct((P, T, N), F32),
        compiler_params=_params("parallel", "parallel", "arbitrary"),
        name="rkv_proj",
    )(a, w)


def _lora_kernel(a_ref, w1_ref, w2_ref, *rest, kind):
    o_ref = rest[-1]
    t = _dot(a_ref[...], w1_ref[...])
    if kind == "decay":
        t = jnp.tanh(t)
    elif kind == "gate":
        t = jax.nn.sigmoid(t)
    z = _dot(t.astype(BF16), w2_ref[...])
    if kind != "gate":
        z = z + rest[0][...]
    if kind == "decay":
        z = -math.exp(-0.5) * jax.nn.sigmoid(z)
    elif kind == "rate":
        z = jax.nn.sigmoid(z)
    o_ref[...] = z


def _lora(mixed, slot, w1, w2, bias, kind, *, tm=512):
    _, T, D = mixed.shape
    R = w1.shape[1]
    in_specs = [
        pl.BlockSpec((None, tm, D), lambda i: (slot, i, 0)),
        pl.BlockSpec((D, R), lambda i: (0, 0)),
        pl.BlockSpec((R, D), lambda i: (0, 0)),
    ]
    args = [mixed, w1, w2]
    if bias is not None:
        in_specs.append(pl.BlockSpec((1, D), lambda i: (0, 0)))
        args.append(bias.reshape(1, D))
    return pl.pallas_call(
        functools.partial(_lora_kernel, kind=kind),
        grid=(T // tm,),
        in_specs=in_specs,
        out_specs=pl.BlockSpec((tm, D), lambda i: (i, 0)),
        out_shape=jax.ShapeDtypeStruct((T, D), F32),
        compiler_params=_params("parallel"),
        name="lora_" + kind,
    )(*args)


def _cumsum_rows(tri, x):
    hi = x.astype(BF16)
    r1 = x - hi.astype(F32)
    mid = r1.astype(BF16)
    lo = (r1 - mid.astype(F32)).astype(BF16)
    return _dot(tri, hi) + _dot(tri, mid) + _dot(tri, lo)


def _wkv_kernel(r_ref, k_ref, v_ref, lw_ref, a_ref, g_ref, kkp_ref, kap_ref, rkp_ref, lng_ref, lnb_ref,
                o_ref, s_scr, *, n_chunks):
    C = WKV_CHUNK
    N = RW_HEAD
    heads = s_scr.shape[0]

    @pl.when(pl.program_id(2) == 0)
    def _():
        s_scr[...] = jnp.zeros_like(s_scr)

    row = lax.broadcasted_iota(jnp.int32, (C, C), 0)
    col = lax.broadcasted_iota(jnp.int32, (C, C), 1)
    strict = row > col
    incl = row >= col
    eye = (row == col).astype(F32)
    tri = incl.astype(BF16)
    base_mask = (row // 2) == (col // 2)
    level_masks = []
    s = 2
    while s < C:
        level_masks.append(((row // (2 * s)) == (col // (2 * s))) & ((row // s) != (col // s)))
        s *= 2

    def chunk(c, _):
        t0 = pl.multiple_of(c * C, C)
        sl = pl.ds(t0, C)
        r_all, k_all, v_all = r_ref[sl, :], k_ref[sl, :], v_ref[sl, :]
        a_all = a_ref[sl, :]
        lw_all = lw_ref[sl, :]
        L_all = _cumsum_rows(tri, lw_all)
        Lc_all = L_all[C - 1:C, :]
        k_all_m = k_all * (1.0 + (a_all - 1.0) * kap_ref[...])
        kkr_all = k_all * kkp_ref[...]
        e_nl_all = jnp.exp(-L_all)
        e_end_all = jnp.exp(Lc_all - L_all)
        e_lp_all = jnp.exp(L_all - lw_all)
        rt_all = r_all * jnp.exp(L_all)
        kt_all = k_all_m * e_nl_all
        kend_all = (k_all_m * e_end_all).astype(BF16)
        vb_all = v_all.astype(BF16)
        e_lc_all = jnp.exp(Lc_all)
        hsl = [slice(h * N, (h + 1) * N) for h in range(heads)]
        H = range(heads)
        kk = [kkr_all[:, hs] for hs in hsl]
        kk = [x / jnp.maximum(jnp.sqrt(jnp.sum(x * x, axis=-1, keepdims=True)), 1e-12) for x in kk]
        b = [kk[h] * a_all[:, hsl[h]] for h in H]
        lhs = [jnp.concatenate([-kk[h] * e_lp_all[:, hsl[h]], rt_all[:, hsl[h]]], axis=0).astype(BF16) for h in H]
        rhs = [jnp.concatenate([b[h] * e_nl_all[:, hsl[h]], kt_all[:, hsl[h]]], axis=0).astype(BF16) for h in H]
        G = [_dot_nt(lhs[h], rhs[h]) for h in H]
        S0 = [s_scr[h] for h in H]
        SA = [_dot_nt(lhs[h], S0[h].astype(BF16)) for h in H]
        A_ab = [jnp.where(strict, G[h][:C, :C], 0.0) for h in H]
        A_ak = [jnp.where(strict, G[h][:C, C:], 0.0).astype(BF16) for h in H]
        P_rb = [jnp.where(incl, G[h][C:, :C], 0.0).astype(BF16) for h in H]
        P_rk = [jnp.where(incl, G[h][C:, C:], 0.0).astype(BF16) for h in H]
        W = [SA[h][:C] + _dot(A_ak[h], vb_all[:, hsl[h]]) for h in H]
        Yv = [SA[h][C:] + _dot(P_rk[h], vb_all[:, hsl[h]]) for h in H]
        Sv = [S0[h] * e_lc_all[:, hsl[h]] + _dot_tn(vb_all[:, hsl[h]], kend_all[:, hsl[h]]) for h in H]
        X = [eye + jnp.where(base_mask, A_ab[h], 0.0) for h in H]
        for off in level_masks:
            Xb = [X[h].astype(BF16) for h in H]
            inner = [_dot(jnp.where(off, A_ab[h], 0.0).astype(BF16), Xb[h]).astype(BF16) for h in H]
            X = [X[h] + _dot(Xb[h], inner[h]) for h in H]
        ub = [_dot(X[h].astype(BF16), W[h].astype(BF16)).astype(BF16) for h in H]
        Y = [Yv[h] + _dot(P_rb[h], ub[h]) for h in H]
        for h in H:
            s_scr[h] = Sv[h] + _dot_tn(ub[h], (b[h] * e_end_all[:, hsl[h]]).astype(BF16))
        rk_all = r_all * k_all_m * rkp_ref[...]
        for h in H:
            hs = hsl[h]
            mu = jnp.mean(Y[h], axis=-1, keepdims=True)
            yc = Y[h] - mu
            var = jnp.mean(yc * yc, axis=-1, keepdims=True)
            yn = yc * lax.rsqrt(var + GN_EPS) * lng_ref[:, hs] + lnb_ref[:, hs]
            bonus = jnp.sum(rk_all[:, hs], axis=-1, keepdims=True) * v_all[:, hs]
            o_ref[sl, hs] = ((yn + bonus) * g_ref[sl, hs]).astype(o_ref.dtype)
        return 0

    lax.fori_loop(0, n_chunks, chunk, 0)


def _wkv(rkv, lw, a, g, kkp, kap, rkp, lng, lnb, batch, seq, *, hw=1024, tb=512):
    _, T, D = rkv.shape
    nt = seq // tb
    tok = lambda b, j, t: (b * nt + t, j)
    par = pl.BlockSpec((1, hw), lambda b, j, t: (0, j))
    return pl.pallas_call(
        functools.partial(_wkv_kernel, n_chunks=tb // WKV_CHUNK),
        grid=(batch, D // hw, nt),
        in_specs=[
            pl.BlockSpec((None, tb, hw), lambda b, j, t: (0, b * nt + t, j)),
            pl.BlockSpec((None, tb, hw), lambda b, j, t: (1, b * nt + t, j)),
            pl.BlockSpec((None, tb, hw), lambda b, j, t: (2, b * nt + t, j)),
            pl.BlockSpec((tb, hw), tok),
            pl.BlockSpec((tb, hw), tok),
            pl.BlockSpec((tb, hw), tok),
            par, par, par, par, par,
        ],
        out_specs=pl.BlockSpec((tb, hw), tok),
        out_shape=jax.ShapeDtypeStruct((T, D), BF16),
        scratch_shapes=[pltpu.VMEM((hw // RW_HEAD, RW_HEAD, RW_HEAD), F32)],
        compiler_params=_params("parallel", "parallel", "arbitrary"),
        name="wkv7",
    )(rkv, rkv, rkv, lw, a, g, kkp.reshape(1, D), kap.reshape(1, D), rkp.reshape(1, D),
      lng.reshape(1, D), lnb.reshape(1, D))


def _pad_rank(w1, w2):
    pad = (-w1.shape[1]) % LANES
    return jnp.pad(w1, ((0, 0), (0, pad))), jnp.pad(w2, ((0, pad), (0, 0)))


def kernel(x, g_pre_mix, g_post_mix, g_pre_ffn, g_post_ffn, ffn_w1, ffn_w2, da_wq, da_wk, da_wv, da_wo, da_lambda, da_subln, rw_mix, rw_wr, rw_wk, rw_wv, rw_wo, rw_w0, rw_w1, rw_w2, rw_a0, rw_a1, rw_a2, rw_g1, rw_g2, rw_kk, rw_ka, rw_rk, rw_lnx_g, rw_lnx_b):
    B, S, D = x.shape
    depth = g_pre_mix.shape[0]
    h = x.reshape(B * S, D)
    tabs = _rope_tables(S)
    for i in range(depth):
        j = i // N_MIXERS
        if i % N_MIXERS == 0:
            lambda_init = 0.8 - 0.6 * math.exp(-0.3 * i)
            wqkv = jnp.concatenate([da_wq[j], da_wk[j], da_wv[j]], axis=1).astype(BF16)
            qkv = _qkv_proj(h, g_pre_mix[i], wqkv, tabs, S)
            att = _diff_attention(qkv, da_lambda[j], da_subln[j], lambda_init, B, S)
            h = _proj_norm_res(att, da_wo[j].astype(BF16), g_post_mix[i], h)
        else:
            mix = rw_mix[j][jnp.array([0, 2, 3, 1, 4, 5])]
            mixed = _rwkv_mix(h, g_pre_mix[i], mix, S)
            rkv = _bmm(mixed, jnp.stack([rw_wr[j], rw_wk[j], rw_wv[j]]).astype(BF16))
            w1, w2 = _pad_rank(rw_w1[j], rw_w2[j])
            a1, a2 = _pad_rank(rw_a1[j], rw_a2[j])
            g1, g2 = _pad_rank(rw_g1[j], rw_g2[j])
            lw = _lora(mixed, 3, w1.astype(BF16), w2.astype(BF16), rw_w0[j], "decay")
            rate = _lora(mixed, 4, a1.astype(BF16), a2.astype(BF16), rw_a0[j], "rate")
            gate = _lora(mixed, 5, g1.astype(BF16), g2.astype(BF16), None, "gate")
            y = _wkv(rkv, lw, rate, gate, rw_kk[j], rw_ka[j], rw_rk[j].reshape(-1), rw_lnx_g[j], rw_lnx_b[j], B, S)
            h = _proj_norm_res(y, rw_wo[j].astype(BF16), g_post_mix[i], h)
        h = _ffn(h, g_pre_ffn[i], ffn_w1[i].astype(BF16), ffn_w2[i].astype(BF16), g_post_ffn[i])
    return h.reshape(B, S, D)
```

```python
import functools
import math

import jax
import jax.numpy as jnp
from jax import lax
from jax.experimental import pallas as pl
from jax.experimental.pallas import tpu as pltpu

F32 = jnp.float32
BF16 = jnp.bfloat16

CHUNK = 64
DA_HEADS = 8
DA_HEAD_DIM = 128
DA_V_DIM = 2 * DA_HEAD_DIM
ROT_DIM = DA_HEAD_DIM // 4
ROPE_THETA = 500000.0
DA_SUBLN_EPS = 1e-5
RW_HEAD = 64
GN_EPS = 64e-5
EPS = 1e-6
NEG_INF = -1e30
N_MIXERS = 2

VMEM_LIMIT_BYTES = 52 * 1024 * 1024
LANES = 128
WKV_CHUNK = 64


def _params(*sem):
    return pltpu.CompilerParams(dimension_semantics=sem, vmem_limit_bytes=VMEM_LIMIT_BYTES)


def _rms(x, eps):
    return x * lax.rsqrt(jnp.mean(x * x, axis=-1, keepdims=True) + eps)


def _dot(a, b):
    return jnp.dot(a, b, preferred_element_type=F32)


def _dot_nt(a, b):
    return lax.dot_general(a, b, (((1,), (1,)), ((), ())), preferred_element_type=F32)


def _dot_tn(a, b):
    return lax.dot_general(a, b, (((0,), (0,)), ((), ())), preferred_element_type=F32)


def _qkv_kernel(x_ref, g_ref, w_ref, tab_ref, o_ref, a_scr, *, n_sub):
    @pl.when(pl.program_id(1) == 0)
    def _():
        a_scr[...] = (_rms(x_ref[...], EPS) * g_ref[...]).astype(BF16)

    c, s1, s2 = tab_ref[0], tab_ref[1], tab_ref[2]
    half = ROT_DIM // 2
    for n in range(o_ref.shape[1] // n_sub):
        acc = _dot(a_scr[...], w_ref[:, n * n_sub:(n + 1) * n_sub])
        for h in range(n_sub // DA_HEAD_DIM):
            blk = acc[:, h * DA_HEAD_DIM:(h + 1) * DA_HEAD_DIM]
            up = pltpu.roll(blk, DA_HEAD_DIM - half, axis=1)
            dn = pltpu.roll(blk, half, axis=1)
            lanes = slice(n * n_sub + h * DA_HEAD_DIM, n * n_sub + (h + 1) * DA_HEAD_DIM)
            o_ref[:, lanes] = (blk * c + up * s1 + dn * s2).astype(o_ref.dtype)


def _qkv_proj(x, g, w, tabs, seq, *, tm=1024, tn=1024, n_sub=256):
    T, D = x.shape
    N = w.shape[1]
    n_q_tiles = (N // 3) // tn
    tiles_per_seq = seq // tm
    return pl.pallas_call(
        functools.partial(_qkv_kernel, n_sub=n_sub),
        grid=(T // tm, N // tn),
        in_specs=[
            pl.BlockSpec((tm, D), lambda i, j: (i, 0)),
            pl.BlockSpec((1, D), lambda i, j: (0, 0)),
            pl.BlockSpec((D, tn), lambda i, j: (0, j)),
            pl.BlockSpec((None, 3, tm, DA_HEAD_DIM),
                         lambda i, j: (j // n_q_tiles, 0, i % tiles_per_seq, 0)),
        ],
        out_specs=pl.BlockSpec((tm, tn), lambda i, j: (i, j)),
        out_shape=jax.ShapeDtypeStruct((T, N), BF16),
        scratch_shapes=[pltpu.VMEM((tm, D), BF16)],
        compiler_params=_params("parallel", "arbitrary"),
        name="qkv_proj",
    )(x, g.reshape(1, D), w, tabs)


def _rope_tables(seq):
    half = ROT_DIM // 2
    inv_freq = ROPE_THETA ** (-jnp.arange(half, dtype=F32) * (2.0 / ROT_DIM))
    ang = jnp.arange(seq, dtype=F32)[:, None] * inv_freq[None, :]
    cos, sin = jnp.cos(ang), jnp.sin(ang)
    rest = DA_HEAD_DIM - ROT_DIM
    c = jnp.concatenate([cos, cos, jnp.ones((seq, rest), F32)], axis=1)
    s1 = jnp.concatenate([-sin, jnp.zeros((seq, DA_HEAD_DIM - half), F32)], axis=1)
    s2 = jnp.concatenate([jnp.zeros((seq, half), F32), sin, jnp.zeros((seq, rest), F32)], axis=1)
    k_tab = jnp.stack([c, s1, s2])
    v_tab = jnp.stack([jnp.ones_like(c), jnp.zeros_like(c), jnp.zeros_like(c)])
    return jnp.stack([k_tab * (DA_HEAD_DIM ** -0.5), k_tab, v_tab])


def _attn_kernel(lam_ref, q_ref, k_ref, v_ref, g_ref, o_ref, *, blk, lambda_init):
    qi = pl.program_id(2)
    row = lax.broadcasted_iota(jnp.int32, (blk, blk), 0)
    col = lax.broadcasted_iota(jnp.int32, (blk, blk), 1)
    allowed = (col // CHUNK) <= (row // CHUNK)
    lv = lam_ref[...]
    lam = (jnp.exp(jnp.sum(lv[0:1] * lv[1:2], axis=-1, keepdims=True))
           - jnp.exp(jnp.sum(lv[2:3] * lv[3:4], axis=-1, keepdims=True)) + lambda_init)

    def softmax_pv(m, n):
        hs = slice(m * DA_HEAD_DIM, (m + 1) * DA_HEAD_DIM)
        q = q_ref[:, hs]
        past = n * blk
        s_d = jnp.where(allowed, _dot_nt(q, k_ref[past:past + blk, hs]), NEG_INF)
        mx = jnp.max(s_d, axis=-1, keepdims=True)
        if n > 0:
            s_p = _dot_nt(q, k_ref[0:past, hs])
            mx = jnp.maximum(mx, jnp.max(s_p, axis=-1, keepdims=True))
        p_d = jnp.exp(s_d - mx)
        l = jnp.sum(p_d, axis=-1, keepdims=True)
        acc = _dot(p_d.astype(BF16), v_ref[past:past + blk, :])
        if n > 0:
            p_p = jnp.exp(s_p - mx)
            l = l + jnp.sum(p_p, axis=-1, keepdims=True)
            acc = acc + _dot(p_p.astype(BF16), v_ref[0:past, :])
        return acc * (1.0 / l)

    for n in range(k_ref.shape[0] // blk):
        @pl.when(qi == n)
        def _(n=n):
            o = softmax_pv(0, n) - lam * softmax_pv(1, n)
            o = _rms(o, DA_SUBLN_EPS) * g_ref[...] * (1.0 - lambda_init)
            o_ref[...] = o.astype(o_ref.dtype)


def _diff_attention(qkv, lam_vecs, subln_g, lambda_init, batch, seq, *, blk=256):
    T = qkv.shape[0]
    nq = seq // blk
    return pl.pallas_call(
        functools.partial(_attn_kernel, blk=blk, lambda_init=lambda_init),
        grid=(batch, DA_HEADS, nq),
        in_specs=[
            pl.BlockSpec((4, DA_HEAD_DIM), lambda b, h, i: (0, 0)),
            pl.BlockSpec((blk, DA_V_DIM), lambda b, h, i: (b * nq + i, h)),
            pl.BlockSpec((seq, DA_V_DIM), lambda b, h, i: (b, DA_HEADS + h)),
            pl.BlockSpec((seq, DA_V_DIM), lambda b, h, i: (b, 2 * DA_HEADS + h)),
            pl.BlockSpec((1, DA_V_DIM), lambda b, h, i: (0, 0)),
        ],
        out_specs=pl.BlockSpec((blk, DA_V_DIM), lambda b, h, i: (b * nq + i, h)),
        out_shape=jax.ShapeDtypeStruct((T, DA_HEADS * DA_V_DIM), BF16),
        compiler_params=_params("parallel", "parallel", "arbitrary"),
        name="diff_attn",
    )(lam_vecs, qkv, qkv, qkv, subln_g.reshape(1, DA_V_DIM))


def _proj_norm_res_kernel(a_ref, w_ref, g_ref, x_ref, o_ref):
    m = _dot(a_ref[...], w_ref[...])
    o_ref[...] = x_ref[...] + _rms(m, EPS) * g_ref[...]


def _proj_norm_res(a, w, g, resid, *, tm=512):
    T, K = a.shape
    N = w.shape[1]
    return pl.pallas_call(
        _proj_norm_res_kernel,
        grid=(T // tm,),
        in_specs=[
            pl.BlockSpec((tm, K), lambda i: (i, 0)),
            pl.BlockSpec((K, N), lambda i: (0, 0)),
            pl.BlockSpec((1, N), lambda i: (0, 0)),
            pl.BlockSpec((tm, N), lambda i: (i, 0)),
        ],
        out_specs=pl.BlockSpec((tm, N), lambda i: (i, 0)),
        out_shape=jax.ShapeDtypeStruct((T, N), F32),
        compiler_params=_params("parallel"),
        name="proj_norm_res",
    )(a, w, g.reshape(1, N), resid)


def _ffn_kernel(x_ref, g1_ref, w1_ref, w2_ref, g2_ref, o_ref, h_scr, acc_scr):
    f = pl.program_id(1)

    @pl.when(f == 0)
    def _():
        h_scr[...] = (_rms(x_ref[...], EPS) * g1_ref[...]).astype(BF16)
        acc_scr[...] = jnp.zeros_like(acc_scr)

    u = jnp.maximum(_dot(h_scr[...], w1_ref[...]), 0.0)
    acc_scr[...] += _dot((u * u).astype(BF16), w2_ref[...])

    @pl.when(f == pl.num_programs(1) - 1)
    def _():
        o_ref[...] = x_ref[...] + _rms(acc_scr[...], EPS) * g2_ref[...]


def _ffn(x, g1, w1, w2, g2, *, tm=512, tf=512):
    T, D = x.shape
    F = w1.shape[1]
    return pl.pallas_call(
        _ffn_kernel,
        grid=(T // tm, F // tf),
        in_specs=[
            pl.BlockSpec((tm, D), lambda i, f: (i, 0)),
            pl.BlockSpec((1, D), lambda i, f: (0, 0)),
            pl.BlockSpec((D, tf), lambda i, f: (0, f)),
            pl.BlockSpec((tf, D), lambda i, f: (f, 0)),
            pl.BlockSpec((1, D), lambda i, f: (0, 0)),
        ],
        out_specs=pl.BlockSpec((tm, D), lambda i, f: (i, 0)),
        out_shape=jax.ShapeDtypeStruct((T, D), F32),
        scratch_shapes=[pltpu.VMEM((tm, D), BF16), pltpu.VMEM((tm, D), F32)],
        compiler_params=_params("parallel", "arbitrary"),
        name="ffn",
    )(x, g1.reshape(1, D), w1, w2, g2.reshape(1, D))


def _mix_kernel(x_ref, xp_ref, g_ref, mix_ref, o_ref, *, tiles_per_seq):
    i = pl.program_id(0)
    g = g_ref[...]
    hn = _rms(x_ref[...], EPS) * g
    sub = xp_ref.shape[0]
    hp = _rms(xp_ref[sub - 1:sub, :], EPS) * g
    hp = jnp.where(i % tiles_per_seq == 0, jnp.zeros_like(hp), hp)
    row = lax.broadcasted_iota(jnp.int32, (hn.shape[0], 1), 0)
    prev = jnp.where(row == 0, hp, pltpu.roll(hn, 1, axis=0))
    xx = prev - hn
    for c in range(o_ref.shape[0]):
        o_ref[c] = (hn + xx * mix_ref[c:c + 1, :]).astype(o_ref.dtype)


def _rwkv_mix(x, g, mix, seq, *, tm=256, sub=8):
    T, D = x.shape
    n_mix = mix.shape[0]
    per = tm // sub
    return pl.pallas_call(
        functools.partial(_mix_kernel, tiles_per_seq=seq // tm),
        grid=(T // tm,),
        in_specs=[
            pl.BlockSpec((tm, D), lambda i: (i, 0)),
            pl.BlockSpec((sub, D), lambda i: (jnp.maximum(i * per - 1, 0), 0)),
            pl.BlockSpec((1, D), lambda i: (0, 0)),
            pl.BlockSpec((n_mix, D), lambda i: (0, 0)),
        ],
        out_specs=pl.BlockSpec((n_mix, tm, D), lambda i: (0, i, 0)),
        out_shape=jax.ShapeDtypeStruct((n_mix, T, D), BF16),
        compiler_params=_params("parallel"),
        name="rwkv_mix",
    )(x, x, g.reshape(1, D), mix)


def _bmm_kernel(a_ref, w_ref, o_ref):
    o_ref[...] = _dot(a_ref[...], w_ref[...])


def _bmm(a, w, *, tm=1024, tn=512):
    P, K, N = w.shape
    T = a.shape[1]
    return pl.pallas_call(
        _bmm_kernel,
        grid=(P, T // tm, N // tn),
        in_specs=[
            pl.BlockSpec((None, tm, K), lambda p, i, j: (p, i, 0)),
            pl.BlockSpec((None, K, tn), lambda p, i, j: (p, 0, j)),
        ],
        out_specs=pl.BlockSpec((None, tm, tn), lambda p, i, j: (p, i, j)),
        out_shape=jax.ShapeDtypeStruct((P, T, N), F32),
        compiler_params=_params("parallel", "parallel", "arbitrary"),
        name="rkv_proj",
    )(a, w)


def _lora_kernel(a_ref, w1_ref, w2_ref, *rest, kind):
    o_ref = rest[-1]
    t = _dot(a_ref[...], w1_ref[...])
    if kind == "decay":
        t = jnp.tanh(t)
    elif kind == "gate":
        t = jax.nn.sigmoid(t)
    z = _dot(t.astype(BF16), w2_ref[...])
    if kind != "gate":
        z = z + rest[0][...]
    if kind == "decay":
        z = -math.exp(-0.5) * jax.nn.sigmoid(z)
    elif kind == "rate":
        z = jax.nn.sigmoid(z)
    o_ref[...] = z


def _lora(mixed, slot, w1, w2, bias, kind, *, tm=512):
    _, T, D = mixed.shape
    R = w1.shape[1]
    in_specs = [
        pl.BlockSpec((None, tm, D), lambda i: (slot, i, 0)),
        pl.BlockSpec((D, R), lambda i: (0, 0)),
        pl.BlockSpec((R, D), lambda i: (0, 0)),
    ]
    args = [mixed, w1, w2]
    if bias is not None:
        in_specs.append(pl.BlockSpec((1, D), lambda i: (0, 0)))
        args.append(bias.reshape(1, D))
    return pl.pallas_call(
        functools.partial(_lora_kernel, kind=kind),
        grid=(T // tm,),
        in_specs=in_specs,
        out_specs=pl.BlockSpec((tm, D), lambda i: (i, 0)),
        out_shape=jax.ShapeDtypeStruct((T, D), F32),
        compiler_params=_params("parallel"),
        name="lora_" + kind,
    )(*args)


def _cumsum_rows(tri, x):
    hi = x.astype(BF16)
    r1 = x - hi.astype(F32)
    mid = r1.astype(BF16)
    lo = (r1 - mid.astype(F32)).astype(BF16)
    return _dot(tri, hi) + _dot(tri, mid) + _dot(tri, lo)


def _wkv_kernel(r_ref, k_ref, v_ref, lw_ref, a_ref, g_ref, kkp_ref, kap_ref, rkp_ref, lng_ref, lnb_ref,
                o_ref, s_scr, *, n_chunks):
    C = WKV_CHUNK
    N = RW_HEAD
    PW = 2 * N
    pairs = s_scr.shape[0]

    @pl.when(pl.program_id(2) == 0)
    def _():
        s_scr[...] = jnp.zeros_like(s_scr)

    row = lax.broadcasted_iota(jnp.int32, (C, PW), 0)
    lane = lax.broadcasted_iota(jnp.int32, (C, PW), 1)
    col = lane % N
    lo = lane < N
    strict = row > col
    incl = row >= col
    incl2 = jnp.concatenate([incl, incl], axis=1)
    eye = (row == col).astype(F32)
    tri = (lax.broadcasted_iota(jnp.int32, (C, C), 0) >= lax.broadcasted_iota(jnp.int32, (C, C), 1)).astype(BF16)
    base_f = ((row // 2) == (col // 2)).astype(F32)
    level_f = []
    s = 2
    while s < C:
        level_f.append((((row // (2 * s)) == (col // (2 * s))) & ((row // s) != (col // s))).astype(F32))
        s *= 2
    diag_blocks = ((lax.broadcasted_iota(jnp.int32, (PW, PW), 0) < N)
                   == (lax.broadcasted_iota(jnp.int32, (PW, PW), 1) < N))

    def bd(z):
        zero = jnp.zeros_like(z)
        return jnp.concatenate([jnp.where(lo, z, zero), jnp.where(lo, zero, z)], axis=0)

    def seg_sum(x):
        s_lo = jnp.sum(jnp.where(lo, x, 0.0), axis=-1, keepdims=True)
        s_hi = jnp.sum(jnp.where(lo, 0.0, x), axis=-1, keepdims=True)
        return jnp.where(lo, s_lo, s_hi)

    def chunk(c, _):
        t0 = pl.multiple_of(c * C, C)
        sl = pl.ds(t0, C)
        lw_all = lw_ref[sl, :]
        L_all = _cumsum_rows(tri, lw_all)
        psl = [slice(p * PW, (p + 1) * PW) for p in range(pairs)]
        P = range(pairs)
        r = [r_ref[sl, ps] for ps in psl]
        k = [k_ref[sl, ps] for ps in psl]
        a = [a_ref[sl, ps] for ps in psl]
        vb = [v_ref[sl, ps].astype(BF16) for ps in psl]
        L = [L_all[:, ps] for ps in psl]
        Lc = [x[C - 1:C, :] for x in L]
        km = [k[p] * (1.0 + (a[p] - 1.0) * kap_ref[:, psl[p]]) for p in P]
        kkr = [k[p] * kkp_ref[:, psl[p]] for p in P]
        kk = [kkr[p] * lax.rsqrt(jnp.maximum(seg_sum(kkr[p] * kkr[p]), 1e-24)) for p in P]
        b = [kk[p] * a[p] for p in P]
        e_nl = [jnp.exp(-L[p]) for p in P]
        e_end = [jnp.exp(Lc[p] - L[p]) for p in P]
        lhs = [jnp.concatenate([-kk[p] * jnp.exp(L[p] - lw_all[:, psl[p]]), r[p] * jnp.exp(L[p])],
                               axis=0).astype(BF16) for p in P]
        rhs = [jnp.concatenate([bd((b[p] * e_nl[p]).astype(BF16)), bd((km[p] * e_nl[p]).astype(BF16))],
                               axis=0) for p in P]
        tn_rhs = [jnp.concatenate([(b[p] * e_end[p]).astype(BF16), (km[p] * e_end[p]).astype(BF16)], axis=0)
                  for p in P]
        G = [_dot_nt(lhs[p], rhs[p]) for p in P]
        S0 = [s_scr[p] for p in P]
        SA = [_dot_nt(lhs[p], S0[p].astype(BF16)) for p in P]
        A_ab = [jnp.where(strict, G[p][:C, :PW], 0.0) for p in P]
        A_ak = [jnp.where(strict, G[p][:C, PW:], 0.0).astype(BF16) for p in P]
        P_rbk = [jnp.where(incl2, G[p][C:, :], 0.0).astype(BF16) for p in P]
        vbd = [bd(vb[p]) for p in P]
        W = [SA[p][:C] + _dot(A_ak[p], vbd[p]) for p in P]
        X = [eye + A_ab[p] * base_f for p in P]
        for lvl in level_f:
            Xb = [X[p].astype(BF16) for p in P]
            inner = [_dot((A_ab[p] * lvl).astype(BF16), bd(Xb[p])).astype(BF16) for p in P]
            X = [X[p] + _dot(Xb[p], bd(inner[p])) for p in P]
        ub = [_dot(X[p].astype(BF16), bd(W[p].astype(BF16))).astype(BF16) for p in P]
        Y = [SA[p][C:] + _dot(P_rbk[p], jnp.concatenate([bd(ub[p]), vbd[p]], axis=0)) for p in P]
        for p in P:
            upd = _dot_tn(jnp.concatenate([ub[p], vb[p]], axis=0), tn_rhs[p])
            s_scr[p] = jnp.where(diag_blocks, S0[p] * jnp.exp(Lc[p]) + upd, 0.0)
        for p in P:
            ps = psl[p]
            yc = Y[p] - seg_sum(Y[p]) * (1.0 / N)
            var = seg_sum(yc * yc) * (1.0 / N)
            yn = yc * lax.rsqrt(var + GN_EPS) * lng_ref[:, ps] + lnb_ref[:, ps]
            bonus = seg_sum(r[p] * km[p] * rkp_ref[:, ps]) * v_ref[sl, ps]
            o_ref[sl, ps] = ((yn + bonus) * g_ref[sl, ps]).astype(o_ref.dtype)
        return 0

    lax.fori_loop(0, n_chunks, chunk, 0)


def _wkv(rkv, lw, a, g, kkp, kap, rkp, lng, lnb, batch, seq, *, hw=2048, tb=256):
    _, T, D = rkv.shape
    nt = seq // tb
    tok = lambda b, j, t: (b * nt + t, j)
    par = pl.BlockSpec((1, hw), lambda b, j, t: (0, j))
    return pl.pallas_call(
        functools.partial(_wkv_kernel, n_chunks=tb // WKV_CHUNK),
        grid=(batch, D // hw, nt),
        in_specs=[
            pl.BlockSpec((None, tb, hw), lambda b, j, t: (0, b * nt + t, j)),
            pl.BlockSpec((None, tb, hw), lambda b, j, t: (1, b * nt + t, j)),
            pl.BlockSpec((None, tb, hw), lambda b, j, t: (2, b * nt + t, j)),
            pl.BlockSpec((tb, hw), tok),
            pl.BlockSpec((tb, hw), tok),
            pl.BlockSpec((tb, hw), tok),
            par, par, par, par, par,
        ],
        out_specs=pl.BlockSpec((tb, hw), tok),
        out_shape=jax.ShapeDtypeStruct((T, D), BF16),
        scratch_shapes=[pltpu.VMEM((hw // (2 * RW_HEAD), 2 * RW_HEAD, 2 * RW_HEAD), F32)],
        compiler_params=_params("parallel", "parallel", "arbitrary"),
        name="wkv7",
    )(rkv, rkv, rkv, lw, a, g, kkp.reshape(1, D), kap.reshape(1, D), rkp.reshape(1, D),
      lng.reshape(1, D), lnb.reshape(1, D))


def _pad_rank(w1, w2):
    pad = (-w1.shape[1]) % LANES
    return jnp.pad(w1, ((0, 0), (0, pad))), jnp.pad(w2, ((0, pad), (0, 0)))


def kernel(x, g_pre_mix, g_post_mix, g_pre_ffn, g_post_ffn, ffn_w1, ffn_w2, da_wq, da_wk, da_wv, da_wo, da_lambda, da_subln, rw_mix, rw_wr, rw_wk, rw_wv, rw_wo, rw_w0, rw_w1, rw_w2, rw_a0, rw_a1, rw_a2, rw_g1, rw_g2, rw_kk, rw_ka, rw_rk, rw_lnx_g, rw_lnx_b):
    B, S, D = x.shape
    depth = g_pre_mix.shape[0]
    h = x.reshape(B * S, D)
    tabs = _rope_tables(S)
    for i in range(depth):
        j = i // N_MIXERS
        if i % N_MIXERS == 0:
            lambda_init = 0.8 - 0.6 * math.exp(-0.3 * i)
            wqkv = jnp.concatenate([da_wq[j], da_wk[j], da_wv[j]], axis=1).astype(BF16)
            qkv = _qkv_proj(h, g_pre_mix[i], wqkv, tabs, S)
            att = _diff_attention(qkv, da_lambda[j], da_subln[j], lambda_init, B, S)
            h = _proj_norm_res(att, da_wo[j].astype(BF16), g_post_mix[i], h)
        else:
            mix = rw_mix[j][jnp.array([0, 2, 3, 1, 4, 5])]
            mixed = _rwkv_mix(h, g_pre_mix[i], mix, S)
            rkv = _bmm(mixed, jnp.stack([rw_wr[j], rw_wk[j], rw_wv[j]]).astype(BF16))
            w1, w2 = _pad_rank(rw_w1[j], rw_w2[j])
            a1, a2 = _pad_rank(rw_a1[j], rw_a2[j])
            g1, g2 = _pad_rank(rw_g1[j], rw_g2[j])
            lw = _lora(mixed, 3, w1.astype(BF16), w2.astype(BF16), rw_w0[j], "decay")
            rate = _lora(mixed, 4, a1.astype(BF16), a2.astype(BF16), rw_a0[j], "rate")
            gate = _lora(mixed, 5, g1.astype(BF16), g2.astype(BF16), None, "gate")
            y = _wkv(rkv, lw, rate, gate, rw_kk[j], rw_ka[j], rw_rk[j].reshape(-1), rw_lnx_g[j], rw_lnx_b[j], B, S)
            h = _proj_norm_res(y, rw_wo[j].astype(BF16), g_post_mix[i], h)
        h = _ffn(h, g_pre_ffn[i], ffn_w1[i].astype(BF16), ffn_w2[i].astype(BF16), g_post_ffn[i])
    return h.reshape(B, S, D)
```

```python
import functools
import math

import jax
import jax.numpy as jnp
from jax import lax
from jax.experimental import pallas as pl
from jax.experimental.pallas import tpu as pltpu

F32 = jnp.float32
BF16 = jnp.bfloat16

CHUNK = 64
DA_HEADS = 8
DA_HEAD_DIM = 128
DA_V_DIM = 2 * DA_HEAD_DIM
ROT_DIM = DA_HEAD_DIM // 4
ROPE_THETA = 500000.0
DA_SUBLN_EPS = 1e-5
RW_HEAD = 64
GN_EPS = 64e-5
EPS = 1e-6
NEG_INF = -1e30
N_MIXERS = 2

VMEM_LIMIT_BYTES = 52 * 1024 * 1024
LANES = 128
WKV_CHUNK = 64


def _params(*sem):
    return pltpu.CompilerParams(dimension_semantics=sem, vmem_limit_bytes=VMEM_LIMIT_BYTES)


def _rms(x, eps):
    return x * lax.rsqrt(jnp.mean(x * x, axis=-1, keepdims=True) + eps)


def _dot(a, b):
    return jnp.dot(a, b, preferred_element_type=F32)


def _dot_nt(a, b):
    return lax.dot_general(a, b, (((1,), (1,)), ((), ())), preferred_element_type=F32)


def _dot_tn(a, b):
    return lax.dot_general(a, b, (((0,), (0,)), ((), ())), preferred_element_type=F32)


def _qkv_kernel(x_ref, g_ref, w_ref, tab_ref, o_ref, a_scr, *, n_sub):
    @pl.when(pl.program_id(1) == 0)
    def _():
        a_scr[...] = (_rms(x_ref[...], EPS) * g_ref[...]).astype(BF16)

    c, s1, s2 = tab_ref[0], tab_ref[1], tab_ref[2]
    half = ROT_DIM // 2
    for n in range(o_ref.shape[1] // n_sub):
        acc = _dot(a_scr[...], w_ref[:, n * n_sub:(n + 1) * n_sub])
        for h in range(n_sub // DA_HEAD_DIM):
            blk = acc[:, h * DA_HEAD_DIM:(h + 1) * DA_HEAD_DIM]
            up = pltpu.roll(blk, DA_HEAD_DIM - half, axis=1)
            dn = pltpu.roll(blk, half, axis=1)
            lanes = slice(n * n_sub + h * DA_HEAD_DIM, n * n_sub + (h + 1) * DA_HEAD_DIM)
            o_ref[:, lanes] = (blk * c + up * s1 + dn * s2).astype(o_ref.dtype)


def _qkv_proj(x, g, w, tabs, seq, *, tm=1024, tn=1024, n_sub=256):
    T, D = x.shape
    N = w.shape[1]
    n_q_tiles = (N // 3) // tn
    tiles_per_seq = seq // tm
    return pl.pallas_call(
        functools.partial(_qkv_kernel, n_sub=n_sub),
        grid=(T // tm, N // tn),
        in_specs=[
            pl.BlockSpec((tm, D), lambda i, j: (i, 0)),
            pl.BlockSpec((1, D), lambda i, j: (0, 0)),
            pl.BlockSpec((D, tn), lambda i, j: (0, j)),
            pl.BlockSpec((None, 3, tm, DA_HEAD_DIM),
                         lambda i, j: (j // n_q_tiles, 0, i % tiles_per_seq, 0)),
        ],
        out_specs=pl.BlockSpec((tm, tn), lambda i, j: (i, j)),
        out_shape=jax.ShapeDtypeStruct((T, N), BF16),
        scratch_shapes=[pltpu.VMEM((tm, D), BF16)],
        compiler_params=_params("parallel", "arbitrary"),
        name="qkv_proj",
    )(x, g.reshape(1, D), w, tabs)


def _rope_tables(seq):
    half = ROT_DIM // 2
    inv_freq = ROPE_THETA ** (-jnp.arange(half, dtype=F32) * (2.0 / ROT_DIM))
    ang = jnp.arange(seq, dtype=F32)[:, None] * inv_freq[None, :]
    cos, sin = jnp.cos(ang), jnp.sin(ang)
    rest = DA_HEAD_DIM - ROT_DIM
    c = jnp.concatenate([cos, cos, jnp.ones((seq, rest), F32)], axis=1)
    s1 = jnp.concatenate([-sin, jnp.zeros((seq, DA_HEAD_DIM - half), F32)], axis=1)
    s2 = jnp.concatenate([jnp.zeros((seq, half), F32), sin, jnp.zeros((seq, rest), F32)], axis=1)
    k_tab = jnp.stack([c, s1, s2])
    v_tab = jnp.stack([jnp.ones_like(c), jnp.zeros_like(c), jnp.zeros_like(c)])
    return jnp.stack([k_tab * (DA_HEAD_DIM ** -0.5), k_tab, v_tab])


def _attn_kernel(lam_ref, q_ref, k_ref, v_ref, g_ref, o_ref, *, blk, lambda_init):
    row = lax.broadcasted_iota(jnp.int32, (blk, blk), 0)
    col = lax.broadcasted_iota(jnp.int32, (blk, blk), 1)
    allowed = (col // CHUNK) <= (row // CHUNK)
    lv = lam_ref[...]
    lam = (jnp.exp(jnp.sum(lv[0:1] * lv[1:2], axis=-1, keepdims=True))
           - jnp.exp(jnp.sum(lv[2:3] * lv[3:4], axis=-1, keepdims=True)) + lambda_init)

    def softmax_pv(m, n):
        hs = slice(m * DA_HEAD_DIM, (m + 1) * DA_HEAD_DIM)
        past = n * blk
        q = q_ref[past:past + blk, hs]
        s_d = jnp.where(allowed, _dot_nt(q, k_ref[past:past + blk, hs]), NEG_INF)
        mx = jnp.max(s_d, axis=-1, keepdims=True)
        if n > 0:
            s_p = _dot_nt(q, k_ref[0:past, hs])
            mx = jnp.maximum(mx, jnp.max(s_p, axis=-1, keepdims=True))
        p_d = jnp.exp(s_d - mx)
        l = jnp.sum(p_d, axis=-1, keepdims=True)
        acc = _dot(p_d.astype(BF16), v_ref[past:past + blk, :])
        if n > 0:
            p_p = jnp.exp(s_p - mx)
            l = l + jnp.sum(p_p, axis=-1, keepdims=True)
            acc = acc + _dot(p_p.astype(BF16), v_ref[0:past, :])
        return acc * (1.0 / l)

    for n in range(k_ref.shape[0] // blk):
        o = softmax_pv(0, n) - lam * softmax_pv(1, n)
        o = _rms(o, DA_SUBLN_EPS) * g_ref[...] * (1.0 - lambda_init)
        o_ref[n * blk:(n + 1) * blk, :] = o.astype(o_ref.dtype)


def _diff_attention(qkv, lam_vecs, subln_g, lambda_init, batch, seq, *, blk=256):
    T = qkv.shape[0]
    return pl.pallas_call(
        functools.partial(_attn_kernel, blk=blk, lambda_init=lambda_init),
        grid=(batch, DA_HEADS),
        in_specs=[
            pl.BlockSpec((4, DA_HEAD_DIM), lambda b, h: (0, 0)),
            pl.BlockSpec((seq, DA_V_DIM), lambda b, h: (b, h)),
            pl.BlockSpec((seq, DA_V_DIM), lambda b, h: (b, DA_HEADS + h)),
            pl.BlockSpec((seq, DA_V_DIM), lambda b, h: (b, 2 * DA_HEADS + h)),
            pl.BlockSpec((1, DA_V_DIM), lambda b, h: (0, 0)),
        ],
        out_specs=pl.BlockSpec((seq, DA_V_DIM), lambda b, h: (b, h)),
        out_shape=jax.ShapeDtypeStruct((T, DA_HEADS * DA_V_DIM), BF16),
        compiler_params=_params("parallel", "parallel"),
        name="diff_attn",
    )(lam_vecs, qkv, qkv, qkv, subln_g.reshape(1, DA_V_DIM))


def _proj_norm_res_kernel(a_ref, w_ref, g_ref, x_ref, o_ref):
    m = _dot(a_ref[...], w_ref[...])
    o_ref[...] = x_ref[...] + _rms(m, EPS) * g_ref[...]


def _proj_norm_res(a, w, g, resid, *, tm=512):
    T, K = a.shape
    N = w.shape[1]
    return pl.pallas_call(
        _proj_norm_res_kernel,
        grid=(T // tm,),
        in_specs=[
            pl.BlockSpec((tm, K), lambda i: (i, 0)),
            pl.BlockSpec((K, N), lambda i: (0, 0)),
            pl.BlockSpec((1, N), lambda i: (0, 0)),
            pl.BlockSpec((tm, N), lambda i: (i, 0)),
        ],
        out_specs=pl.BlockSpec((tm, N), lambda i: (i, 0)),
        out_shape=jax.ShapeDtypeStruct((T, N), F32),
        compiler_params=_params("parallel"),
        name="proj_norm_res",
    )(a, w, g.reshape(1, N), resid)


def _ffn_kernel(x_ref, g1_ref, w1_ref, w2_ref, g2_ref, o_ref, h_scr, acc_scr):
    f = pl.program_id(1)

    @pl.when(f == 0)
    def _():
        h_scr[...] = (_rms(x_ref[...], EPS) * g1_ref[...]).astype(BF16)
        acc_scr[...] = jnp.zeros_like(acc_scr)

    u = jnp.maximum(_dot(h_scr[...], w1_ref[...]), 0.0)
    acc_scr[...] += _dot((u * u).astype(BF16), w2_ref[...])

    @pl.when(f == pl.num_programs(1) - 1)
    def _():
        o_ref[...] = x_ref[...] + _rms(acc_scr[...], EPS) * g2_ref[...]


def _ffn(x, g1, w1, w2, g2, *, tm=512, tf=1024):
    T, D = x.shape
    F = w1.shape[1]
    return pl.pallas_call(
        _ffn_kernel,
        grid=(T // tm, F // tf),
        in_specs=[
            pl.BlockSpec((tm, D), lambda i, f: (i, 0)),
            pl.BlockSpec((1, D), lambda i, f: (0, 0)),
            pl.BlockSpec((D, tf), lambda i, f: (0, f)),
            pl.BlockSpec((tf, D), lambda i, f: (f, 0)),
            pl.BlockSpec((1, D), lambda i, f: (0, 0)),
        ],
        out_specs=pl.BlockSpec((tm, D), lambda i, f: (i, 0)),
        out_shape=jax.ShapeDtypeStruct((T, D), F32),
        scratch_shapes=[pltpu.VMEM((tm, D), BF16), pltpu.VMEM((tm, D), F32)],
        compiler_params=_params("parallel", "arbitrary"),
        name="ffn",
    )(x, g1.reshape(1, D), w1, w2, g2.reshape(1, D))


def _mix_kernel(x_ref, xp_ref, g_ref, mix_ref, o_ref, *, tiles_per_seq):
    i = pl.program_id(0)
    g = g_ref[...]
    hn = _rms(x_ref[...], EPS) * g
    sub = xp_ref.shape[0]
    hp = _rms(xp_ref[sub - 1:sub, :], EPS) * g
    hp = jnp.where(i % tiles_per_seq == 0, jnp.zeros_like(hp), hp)
    row = lax.broadcasted_iota(jnp.int32, (hn.shape[0], 1), 0)
    prev = jnp.where(row == 0, hp, pltpu.roll(hn, 1, axis=0))
    xx = prev - hn
    for c in range(o_ref.shape[0]):
        o_ref[c] = (hn + xx * mix_ref[c:c + 1, :]).astype(o_ref.dtype)


def _rwkv_mix(x, g, mix, seq, *, tm=256, sub=8):
    T, D = x.shape
    n_mix = mix.shape[0]
    per = tm // sub
    return pl.pallas_call(
        functools.partial(_mix_kernel, tiles_per_seq=seq // tm),
        grid=(T // tm,),
        in_specs=[
            pl.BlockSpec((tm, D), lambda i: (i, 0)),
            pl.BlockSpec((sub, D), lambda i: (jnp.maximum(i * per - 1, 0), 0)),
            pl.BlockSpec((1, D), lambda i: (0, 0)),
            pl.BlockSpec((n_mix, D), lambda i: (0, 0)),
        ],
        out_specs=pl.BlockSpec((n_mix, tm, D), lambda i: (0, i, 0)),
        out_shape=jax.ShapeDtypeStruct((n_mix, T, D), BF16),
        compiler_params=_params("parallel"),
        name="rwkv_mix",
    )(x, x, g.reshape(1, D), mix)


def _bmm_kernel(a_ref, w_ref, o_ref):
    o_ref[...] = _dot(a_ref[...], w_ref[...])


def _bmm(a, w, *, tm=1024, tn=1024):
    P, K, N = w.shape
    T = a.shape[1]
    return pl.pallas_call(
        _bmm_kernel,
        grid=(P, T // tm, N // tn),
        in_specs=[
            pl.BlockSpec((None, tm, K), lambda p, i, j: (p, i, 0)),
            pl.BlockSpec((None, K, tn), lambda p, i, j: (p, 0, j)),
        ],
        out_specs=pl.BlockSpec((None, tm, tn), lambda p, i, j: (p, i, j)),
        out_shape=jax.ShapeDtypeStruct((P, T, N), F32),
        compiler_params=_params("parallel", "parallel", "arbitrary"),
        name="rkv_proj",
    )(a, w)


def _lora_kernel(a_ref, w1_ref, w2_ref, *rest, kind):
    o_ref = rest[-1]
    t = _dot(a_ref[...], w1_ref[...])
    if kind == "decay":
        t = jnp.tanh(t)
    elif kind == "gate":
        t = jax.nn.sigmoid(t)
    z = _dot(t.astype(BF16), w2_ref[...])
    if kind != "gate":
        z = z + rest[0][...]
    if kind == "decay":
        z = -math.exp(-0.5) * jax.nn.sigmoid(z)
    elif kind == "rate":
        z = jax.nn.sigmoid(z)
    o_ref[...] = z


def _lora(mixed, slot, w1, w2, bias, kind, *, tm=512):
    _, T, D = mixed.shape
    R = w1.shape[1]
    in_specs = [
        pl.BlockSpec((None, tm, D), lambda i: (slot, i, 0)),
        pl.BlockSpec((D, R), lambda i: (0, 0)),
        pl.BlockSpec((R, D), lambda i: (0, 0)),
    ]
    args = [mixed, w1, w2]
    if bias is not None:
        in_specs.append(pl.BlockSpec((1, D), lambda i: (0, 0)))
        args.append(bias.reshape(1, D))
    return pl.pallas_call(
        functools.partial(_lora_kernel, kind=kind),
        grid=(T // tm,),
        in_specs=in_specs,
        out_specs=pl.BlockSpec((tm, D), lambda i: (i, 0)),
        out_shape=jax.ShapeDtypeStruct((T, D), F32),
        compiler_params=_params("parallel"),
        name="lora_" + kind,
    )(*args)


def _cumsum_rows(tri, x):
    hi = x.astype(BF16)
    r1 = x - hi.astype(F32)
    mid = r1.astype(BF16)
    lo = (r1 - mid.astype(F32)).astype(BF16)
    return _dot(tri, hi) + _dot(tri, mid) + _dot(tri, lo)


def _wkv_kernel(r_ref, k_ref, v_ref, lw_ref, a_ref, g_ref, kkp_ref, kap_ref, rkp_ref, lng_ref, lnb_ref,
                o_ref, s_scr, *, n_chunks):
    C = WKV_CHUNK
    N = RW_HEAD
    PW = 2 * N
    pairs = s_scr.shape[0]

    @pl.when(pl.program_id(2) == 0)
    def _():
        s_scr[...] = jnp.zeros_like(s_scr)

    row = lax.broadcasted_iota(jnp.int32, (C, PW), 0)
    lane = lax.broadcasted_iota(jnp.int32, (C, PW), 1)
    col = lane % N
    lo = lane < N
    strict = row > col
    incl = row >= col
    incl2 = jnp.concatenate([incl, incl], axis=1)
    eye = (row == col).astype(F32)
    tri = (lax.broadcasted_iota(jnp.int32, (C, C), 0) >= lax.broadcasted_iota(jnp.int32, (C, C), 1)).astype(BF16)
    base_f = ((row // 2) == (col // 2)).astype(F32)
    level_f = []
    s = 2
    while s < C:
        level_f.append((((row // (2 * s)) == (col // (2 * s))) & ((row // s) != (col // s))).astype(F32))
        s *= 2
    diag_blocks = ((lax.broadcasted_iota(jnp.int32, (PW, PW), 0) < N)
                   == (lax.broadcasted_iota(jnp.int32, (PW, PW), 1) < N))

    def bd(z):
        zero = jnp.zeros_like(z)
        return jnp.concatenate([jnp.where(lo, z, zero), jnp.where(lo, zero, z)], axis=0)

    def seg_sum(x):
        s_lo = jnp.sum(jnp.where(lo, x, 0.0), axis=-1, keepdims=True)
        s_hi = jnp.sum(jnp.where(lo, 0.0, x), axis=-1, keepdims=True)
        return jnp.where(lo, s_lo, s_hi)

    def chunk(c, _):
        t0 = pl.multiple_of(c * C, C)
        sl = pl.ds(t0, C)
        lw_all = lw_ref[sl, :]
        L_all = _cumsum_rows(tri, lw_all)
        psl = [slice(p * PW, (p + 1) * PW) for p in range(pairs)]
        P = range(pairs)
        r = [r_ref[sl, ps] for ps in psl]
        k = [k_ref[sl, ps] for ps in psl]
        a = [a_ref[sl, ps] for ps in psl]
        vb = [v_ref[sl, ps].astype(BF16) for ps in psl]
        L = [L_all[:, ps] for ps in psl]
        Lc = [x[C - 1:C, :] for x in L]
        km = [k[p] * (1.0 + (a[p] - 1.0) * kap_ref[:, psl[p]]) for p in P]
        kkr = [k[p] * kkp_ref[:, psl[p]] for p in P]
        kk = [kkr[p] * lax.rsqrt(jnp.maximum(seg_sum(kkr[p] * kkr[p]), 1e-24)) for p in P]
        b = [kk[p] * a[p] for p in P]
        e_nl = [jnp.exp(-L[p]) for p in P]
        e_end = [jnp.exp(Lc[p] - L[p]) for p in P]
        lhs = [jnp.concatenate([-kk[p] * jnp.exp(L[p] - lw_all[:, psl[p]]), r[p] * jnp.exp(L[p])],
                               axis=0).astype(BF16) for p in P]
        rhs = [jnp.concatenate([bd((b[p] * e_nl[p]).astype(BF16)), bd((km[p] * e_nl[p]).astype(BF16))],
                               axis=0) for p in P]
        tn_rhs = [jnp.concatenate([(b[p] * e_end[p]).astype(BF16), (km[p] * e_end[p]).astype(BF16)], axis=0)
                  for p in P]
        G = [_dot_nt(lhs[p], rhs[p]) for p in P]
        S0 = [s_scr[p] for p in P]
        SA = [_dot_nt(lhs[p], S0[p].astype(BF16)) for p in P]
        A_ab = [jnp.where(strict, G[p][:C, :PW], 0.0) for p in P]
        A_ak = [jnp.where(strict, G[p][:C, PW:], 0.0).astype(BF16) for p in P]
        P_rbk = [jnp.where(incl2, G[p][C:, :], 0.0).astype(BF16) for p in P]
        vbd = [bd(vb[p]) for p in P]
        W = [SA[p][:C] + _dot(A_ak[p], vbd[p]) for p in P]
        X = [eye + A_ab[p] * base_f for p in P]
        for lvl in level_f:
            Xb = [X[p].astype(BF16) for p in P]
            inner = [_dot((A_ab[p] * lvl).astype(BF16), bd(Xb[p])).astype(BF16) for p in P]
            X = [X[p] + _dot(Xb[p], bd(inner[p])) for p in P]
        ub = [_dot(X[p].astype(BF16), bd(W[p].astype(BF16))).astype(BF16) for p in P]
        Y = [SA[p][C:] + _dot(P_rbk[p], jnp.concatenate([bd(ub[p]), vbd[p]], axis=0)) for p in P]
        for p in P:
            upd = _dot_tn(jnp.concatenate([ub[p], vb[p]], axis=0), tn_rhs[p])
            s_scr[p] = jnp.where(diag_blocks, S0[p] * jnp.exp(Lc[p]) + upd, 0.0)
        for p in P:
            ps = psl[p]
            yc = Y[p] - seg_sum(Y[p]) * (1.0 / N)
            var = seg_sum(yc * yc) * (1.0 / N)
            yn = yc * lax.rsqrt(var + GN_EPS) * lng_ref[:, ps] + lnb_ref[:, ps]
            bonus = seg_sum(r[p] * km[p] * rkp_ref[:, ps]) * v_ref[sl, ps]
            o_ref[sl, ps] = ((yn + bonus) * g_ref[sl, ps]).astype(o_ref.dtype)
        return 0

    lax.fori_loop(0, n_chunks, chunk, 0)


def _wkv(rkv, lw, a, g, kkp, kap, rkp, lng, lnb, batch, seq, *, hw=2048, tb=256):
    _, T, D = rkv.shape
    nt = seq // tb
    tok = lambda b, j, t: (b * nt + t, j)
    par = pl.BlockSpec((1, hw), lambda b, j, t: (0, j))
    return pl.pallas_call(
        functools.partial(_wkv_kernel, n_chunks=tb // WKV_CHUNK),
        grid=(batch, D // hw, nt),
        in_specs=[
            pl.BlockSpec((None, tb, hw), lambda b, j, t: (0, b * nt + t, j)),
            pl.BlockSpec((None, tb, hw), lambda b, j, t: (1, b * nt + t, j)),
            pl.BlockSpec((None, tb, hw), lambda b, j, t: (2, b * nt + t, j)),
            pl.BlockSpec((tb, hw), tok),
            pl.BlockSpec((tb, hw), tok),
            pl.BlockSpec((tb, hw), tok),
            par, par, par, par, par,
        ],
        out_specs=pl.BlockSpec((tb, hw), tok),
        out_shape=jax.ShapeDtypeStruct((T, D), BF16),
        scratch_shapes=[pltpu.VMEM((hw // (2 * RW_HEAD), 2 * RW_HEAD, 2 * RW_HEAD), F32)],
        compiler_params=_params("parallel", "parallel", "arbitrary"),
        name="wkv7",
    )(rkv, rkv, rkv, lw, a, g, kkp.reshape(1, D), kap.reshape(1, D), rkp.reshape(1, D),
      lng.reshape(1, D), lnb.reshape(1, D))


def _pad_rank(w1, w2):
    pad = (-w1.shape[1]) % LANES
    return jnp.pad(w1, ((0, 0), (0, pad))), jnp.pad(w2, ((0, pad), (0, 0)))


def kernel(x, g_pre_mix, g_post_mix, g_pre_ffn, g_post_ffn, ffn_w1, ffn_w2, da_wq, da_wk, da_wv, da_wo, da_lambda, da_subln, rw_mix, rw_wr, rw_wk, rw_wv, rw_wo, rw_w0, rw_w1, rw_w2, rw_a0, rw_a1, rw_a2, rw_g1, rw_g2, rw_kk, rw_ka, rw_rk, rw_lnx_g, rw_lnx_b):
    B, S, D = x.shape
    depth = g_pre_mix.shape[0]
    h = x.reshape(B * S, D)
    tabs = _rope_tables(S)
    for i in range(depth):
        j = i // N_MIXERS
        if i % N_MIXERS == 0:
            lambda_init = 0.8 - 0.6 * math.exp(-0.3 * i)
            wqkv = jnp.concatenate([da_wq[j], da_wk[j], da_wv[j]], axis=1).astype(BF16)
            qkv = _qkv_proj(h, g_pre_mix[i], wqkv, tabs, S)
            att = _diff_attention(qkv, da_lambda[j], da_subln[j], lambda_init, B, S)
            h = _proj_norm_res(att, da_wo[j].astype(BF16), g_post_mix[i], h)
        else:
            mix = rw_mix[j][jnp.array([0, 2, 3, 1, 4, 5])]
            mixed = _rwkv_mix(h, g_pre_mix[i], mix, S)
            rkv = _bmm(mixed, jnp.stack([rw_wr[j], rw_wk[j], rw_wv[j]]).astype(BF16))
            w1, w2 = _pad_rank(rw_w1[j], rw_w2[j])
            a1, a2 = _pad_rank(rw_a1[j], rw_a2[j])
            g1, g2 = _pad_rank(rw_g1[j], rw_g2[j])
            lw = _lora(mixed, 3, w1.astype(BF16), w2.astype(BF16), rw_w0[j], "decay")
            rate = _lora(mixed, 4, a1.astype(BF16), a2.astype(BF16), rw_a0[j], "rate")
            gate = _lora(mixed, 5, g1.astype(BF16), g2.astype(BF16), None, "gate")
            y = _wkv(rkv, lw, rate, gate, rw_kk[j], rw_ka[j], rw_rk[j].reshape(-1), rw_lnx_g[j], rw_lnx_b[j], B, S)
            h = _proj_norm_res(y, rw_wo[j].astype(BF16), g_post_mix[i], h)
        h = _ffn(h, g_pre_ffn[i], ffn_w1[i].astype(BF16), ffn_w2[i].astype(BF16), g_post_ffn[i])
    return h.reshape(B, S, D)
```

```python
import functools
import math

import jax
import jax.numpy as jnp
from jax import lax
from jax.experimental import pallas as pl
from jax.experimental.pallas import tpu as pltpu

F32 = jnp.float32
BF16 = jnp.bfloat16

CHUNK = 64
DA_HEADS = 8
DA_HEAD_DIM = 128
DA_V_DIM = 2 * DA_HEAD_DIM
ROT_DIM = DA_HEAD_DIM // 4
ROPE_THETA = 500000.0
DA_SUBLN_EPS = 1e-5
RW_HEAD = 64
GN_EPS = 64e-5
EPS = 1e-6
NEG_INF = -1e30
N_MIXERS = 2

VMEM_LIMIT_BYTES = 52 * 1024 * 1024
LANES = 128
WKV_CHUNK = 64


def _params(*sem):
    return pltpu.CompilerParams(dimension_semantics=sem, vmem_limit_bytes=VMEM_LIMIT_BYTES)


def _rms(x, eps):
    return x * lax.rsqrt(jnp.mean(x * x, axis=-1, keepdims=True) + eps)


def _dot(a, b):
    return jnp.dot(a, b, preferred_element_type=F32)


def _dot_nt(a, b):
    return lax.dot_general(a, b, (((1,), (1,)), ((), ())), preferred_element_type=F32)


def _dot_tn(a, b):
    return lax.dot_general(a, b, (((0,), (0,)), ((), ())), preferred_element_type=F32)


def _qkv_kernel(x_ref, g_ref, w_ref, tab_ref, o_ref, a_scr, *, n_sub):
    @pl.when(pl.program_id(1) == 0)
    def _():
        a_scr[...] = (_rms(x_ref[...], EPS) * g_ref[...]).astype(BF16)

    c, s1, s2 = tab_ref[0], tab_ref[1], tab_ref[2]
    half = ROT_DIM // 2
    for n in range(o_ref.shape[1] // n_sub):
        acc = _dot(a_scr[...], w_ref[:, n * n_sub:(n + 1) * n_sub])
        for h in range(n_sub // DA_HEAD_DIM):
            blk = acc[:, h * DA_HEAD_DIM:(h + 1) * DA_HEAD_DIM]
            up = pltpu.roll(blk, DA_HEAD_DIM - half, axis=1)
            dn = pltpu.roll(blk, half, axis=1)
            lanes = slice(n * n_sub + h * DA_HEAD_DIM, n * n_sub + (h + 1) * DA_HEAD_DIM)
            o_ref[:, lanes] = (blk * c + up * s1 + dn * s2).astype(o_ref.dtype)


def _qkv_proj(x, g, w, tabs, seq, *, tm=1024, tn=1024, n_sub=256):
    T, D = x.shape
    N = w.shape[1]
    n_q_tiles = (N // 3) // tn
    tiles_per_seq = seq // tm
    return pl.pallas_call(
        functools.partial(_qkv_kernel, n_sub=n_sub),
        grid=(T // tm, N // tn),
        in_specs=[
            pl.BlockSpec((tm, D), lambda i, j: (i, 0)),
            pl.BlockSpec((1, D), lambda i, j: (0, 0)),
            pl.BlockSpec((D, tn), lambda i, j: (0, j)),
            pl.BlockSpec((None, 3, tm, DA_HEAD_DIM),
                         lambda i, j: (j // n_q_tiles, 0, i % tiles_per_seq, 0)),
        ],
        out_specs=pl.BlockSpec((tm, tn), lambda i, j: (i, j)),
        out_shape=jax.ShapeDtypeStruct((T, N), BF16),
        scratch_shapes=[pltpu.VMEM((tm, D), BF16)],
        compiler_params=_params("parallel", "arbitrary"),
        name="qkv_proj",
    )(x, g.reshape(1, D), w, tabs)


def _rope_tables(seq):
    half = ROT_DIM // 2
    inv_freq = ROPE_THETA ** (-jnp.arange(half, dtype=F32) * (2.0 / ROT_DIM))
    ang = jnp.arange(seq, dtype=F32)[:, None] * inv_freq[None, :]
    cos, sin = jnp.cos(ang), jnp.sin(ang)
    rest = DA_HEAD_DIM - ROT_DIM
    c = jnp.concatenate([cos, cos, jnp.ones((seq, rest), F32)], axis=1)
    s1 = jnp.concatenate([-sin, jnp.zeros((seq, DA_HEAD_DIM - half), F32)], axis=1)
    s2 = jnp.concatenate([jnp.zeros((seq, half), F32), sin, jnp.zeros((seq, rest), F32)], axis=1)
    k_tab = jnp.stack([c, s1, s2])
    v_tab = jnp.stack([jnp.ones_like(c), jnp.zeros_like(c), jnp.zeros_like(c)])
    return jnp.stack([k_tab * (DA_HEAD_DIM ** -0.5), k_tab, v_tab])


def _attn_kernel(lam_ref, q_ref, k_ref, v_ref, g_ref, o_ref, *, blk, lambda_init):
    row = lax.broadcasted_iota(jnp.int32, (blk, blk), 0)
    col = lax.broadcasted_iota(jnp.int32, (blk, blk), 1)
    allowed = (col // CHUNK) <= (row // CHUNK)
    lv = lam_ref[...]
    lam = (jnp.exp(jnp.sum(lv[0:1] * lv[1:2], axis=-1, keepdims=True))
           - jnp.exp(jnp.sum(lv[2:3] * lv[3:4], axis=-1, keepdims=True)) + lambda_init)

    def softmax_pv(m, n):
        hs = slice(m * DA_HEAD_DIM, (m + 1) * DA_HEAD_DIM)
        past = n * blk
        q = q_ref[past:past + blk, hs]
        s_d = jnp.where(allowed, _dot_nt(q, k_ref[past:past + blk, hs]), NEG_INF)
        mx = jnp.max(s_d, axis=-1, keepdims=True)
        if n > 0:
            s_p = _dot_nt(q, k_ref[0:past, hs])
            mx = jnp.maximum(mx, jnp.max(s_p, axis=-1, keepdims=True))
        p_d = jnp.exp(s_d - mx)
        l = jnp.sum(p_d, axis=-1, keepdims=True)
        acc = _dot(p_d.astype(BF16), v_ref[past:past + blk, :])
        if n > 0:
            p_p = jnp.exp(s_p - mx)
            l = l + jnp.sum(p_p, axis=-1, keepdims=True)
            acc = acc + _dot(p_p.astype(BF16), v_ref[0:past, :])
        return acc * (1.0 / l)

    for n in range(k_ref.shape[0] // blk):
        o = softmax_pv(0, n) - lam * softmax_pv(1, n)
        o = _rms(o, DA_SUBLN_EPS) * g_ref[...] * (1.0 - lambda_init)
        o_ref[n * blk:(n + 1) * blk, :] = o.astype(o_ref.dtype)


def _diff_attention(qkv, lam_vecs, subln_g, lambda_init, batch, seq, *, blk=256):
    T = qkv.shape[0]
    return pl.pallas_call(
        functools.partial(_attn_kernel, blk=blk, lambda_init=lambda_init),
        grid=(batch, DA_HEADS),
        in_specs=[
            pl.BlockSpec((4, DA_HEAD_DIM), lambda b, h: (0, 0)),
            pl.BlockSpec((seq, DA_V_DIM), lambda b, h: (b, h)),
            pl.BlockSpec((seq, DA_V_DIM), lambda b, h: (b, DA_HEADS + h)),
            pl.BlockSpec((seq, DA_V_DIM), lambda b, h: (b, 2 * DA_HEADS + h)),
            pl.BlockSpec((1, DA_V_DIM), lambda b, h: (0, 0)),
        ],
        out_specs=pl.BlockSpec((seq, DA_V_DIM), lambda b, h: (b, h)),
        out_shape=jax.ShapeDtypeStruct((T, DA_HEADS * DA_V_DIM), BF16),
        compiler_params=_params("parallel", "parallel"),
        name="diff_attn",
    )(lam_vecs, qkv, qkv, qkv, subln_g.reshape(1, DA_V_DIM))


def _proj_norm_res_kernel(a_ref, w_ref, g_ref, x_ref, o_ref):
    m = _dot(a_ref[...], w_ref[...])
    o_ref[...] = x_ref[...] + _rms(m, EPS) * g_ref[...]


def _proj_norm_res(a, w, g, resid, *, tm=512):
    T, K = a.shape
    N = w.shape[1]
    return pl.pallas_call(
        _proj_norm_res_kernel,
        grid=(T // tm,),
        in_specs=[
            pl.BlockSpec((tm, K), lambda i: (i, 0)),
            pl.BlockSpec((K, N), lambda i: (0, 0)),
            pl.BlockSpec((1, N), lambda i: (0, 0)),
            pl.BlockSpec((tm, N), lambda i: (i, 0)),
        ],
        out_specs=pl.BlockSpec((tm, N), lambda i: (i, 0)),
        out_shape=jax.ShapeDtypeStruct((T, N), F32),
        compiler_params=_params("parallel"),
        name="proj_norm_res",
    )(a, w, g.reshape(1, N), resid)


def _ffn_kernel(x_ref, g1_ref, w1_ref, w2_ref, g2_ref, o_ref, h_scr, acc_scr):
    f = pl.program_id(1)

    @pl.when(f == 0)
    def _():
        h_scr[...] = (_rms(x_ref[...], EPS) * g1_ref[...]).astype(BF16)
        acc_scr[...] = jnp.zeros_like(acc_scr)

    u = jnp.maximum(_dot(h_scr[...], w1_ref[...]), 0.0)
    acc_scr[...] += _dot((u * u).astype(BF16), w2_ref[...])

    @pl.when(f == pl.num_programs(1) - 1)
    def _():
        o_ref[...] = x_ref[...] + _rms(acc_scr[...], EPS) * g2_ref[...]


def _ffn(x, g1, w1, w2, g2, *, tm=512, tf=1024):
    T, D = x.shape
    F = w1.shape[1]
    return pl.pallas_call(
        _ffn_kernel,
        grid=(T // tm, F // tf),
        in_specs=[
            pl.BlockSpec((tm, D), lambda i, f: (i, 0)),
            pl.BlockSpec((1, D), lambda i, f: (0, 0)),
            pl.BlockSpec((D, tf), lambda i, f: (0, f)),
            pl.BlockSpec((tf, D), lambda i, f: (f, 0)),
            pl.BlockSpec((1, D), lambda i, f: (0, 0)),
        ],
        out_specs=pl.BlockSpec((tm, D), lambda i, f: (i, 0)),
        out_shape=jax.ShapeDtypeStruct((T, D), F32),
        scratch_shapes=[pltpu.VMEM((tm, D), BF16), pltpu.VMEM((tm, D), F32)],
        compiler_params=_params("parallel", "arbitrary"),
        name="ffn",
    )(x, g1.reshape(1, D), w1, w2, g2.reshape(1, D))


def _rwkv_in_kernel(x_ref, xp_ref, u_ref, w_ref, wrkv_ref, l1_ref, l2_ref, lb_ref, o_ref, rinv_scr, a_scr,
                    *, tiles_per_seq):
    i = pl.program_id(0)
    c = pl.program_id(1)

    def build(cn, slot):
        xs = x_ref[...] * rinv_scr[...]
        sub = xp_ref.shape[0]
        xp = _rms(xp_ref[sub - 1:sub, :], EPS)
        xp = jnp.where(i % tiles_per_seq == 0, jnp.zeros_like(xp), xp)
        row = lax.broadcasted_iota(jnp.int32, (xs.shape[0], 1), 0)
        prev = jnp.where(row == 0, xp, pltpu.roll(xs, 1, axis=0))
        a_scr[slot] = (xs * u_ref[cn:cn + 1, :] + prev * w_ref[cn:cn + 1, :]).astype(BF16)

    @pl.when(c == 0)
    def _():
        x = x_ref[...]
        rinv_scr[...] = lax.rsqrt(jnp.mean(x * x, axis=-1, keepdims=True) + EPS)
        build(0, 0)

    for cc in range(3):
        @pl.when(c == cc)
        def _(cc=cc):
            o_ref[...] = _dot(a_scr[cc % 2], wrkv_ref[...])
            build(cc + 1, (cc + 1) % 2)

    def sigmoid(z):
        return 0.5 * jnp.tanh(0.5 * z) + 0.5

    def low_rank(slot, pre, post):
        halves = [slice(h * (tm // 2), (h + 1) * (tm // 2)) for h in range(2)]
        t = [_dot(a_scr[slot, rows, :], l1_ref[...]) for rows in halves]
        t = [pre(v).astype(BF16) for v in t]
        z = [_dot(v, l2_ref[...]) for v in t]
        for rows, v in zip(halves, z):
            o_ref[rows, :] = post(v)

    tm = o_ref.shape[0]

    @pl.when(c == 3)
    def _():
        low_rank(1, jnp.tanh, lambda z: -math.exp(-0.5) * sigmoid(z + lb_ref[...]))
        build(4, 0)

    @pl.when(c == 4)
    def _():
        low_rank(0, lambda t: t, lambda z: sigmoid(z + lb_ref[...]))
        build(5, 1)

    @pl.when(c == 5)
    def _():
        low_rank(1, sigmoid, lambda z: z)


def _rwkv_in(x, u, w, wrkv, l1, l2, lb, seq, *, tm=512, sub=8):
    T, D = x.shape
    n_proj = u.shape[0]
    n_dense = wrkv.shape[0]
    R = l1.shape[2]
    per = tm // sub
    low = lambda i, c: (jnp.maximum(c - n_dense, 0), 0, 0)
    return pl.pallas_call(
        functools.partial(_rwkv_in_kernel, tiles_per_seq=seq // tm),
        grid=(T // tm, n_proj),
        in_specs=[
            pl.BlockSpec((tm, D), lambda i, c: (i, 0)),
            pl.BlockSpec((sub, D), lambda i, c: (jnp.maximum(i * per - 1, 0), 0)),
            pl.BlockSpec((n_proj, D), lambda i, c: (0, 0)),
            pl.BlockSpec((n_proj, D), lambda i, c: (0, 0)),
            pl.BlockSpec((None, D, D), lambda i, c: (jnp.minimum(c, n_dense - 1), 0, 0)),
            pl.BlockSpec((None, D, R), low),
            pl.BlockSpec((None, R, D), low),
            pl.BlockSpec((None, 1, D), low),
        ],
        out_specs=pl.BlockSpec((None, tm, D), lambda i, c: (c, i, 0)),
        out_shape=jax.ShapeDtypeStruct((n_proj, T, D), F32),
        scratch_shapes=[pltpu.VMEM((tm, 1), F32), pltpu.VMEM((2, tm, D), BF16)],
        compiler_params=_params("parallel", "arbitrary"),
        name="rwkv_in",
    )(x, x, u, w, wrkv, l1, l2, lb)


def _cumsum_rows(tri, x):
    hi = x.astype(BF16)
    r1 = x - hi.astype(F32)
    mid = r1.astype(BF16)
    lo = (r1 - mid.astype(F32)).astype(BF16)
    return _dot(tri, hi) + _dot(tri, mid) + _dot(tri, lo)


def _wkv_kernel(r_ref, k_ref, v_ref, lw_ref, a_ref, g_ref, kkp_ref, kap_ref, rkp_ref, lng_ref, lnb_ref,
                o_ref, s_scr, *, n_chunks):
    C = WKV_CHUNK
    N = RW_HEAD
    PW = 2 * N
    pairs = s_scr.shape[0]

    @pl.when(pl.program_id(2) == 0)
    def _():
        s_scr[...] = jnp.zeros_like(s_scr)

    row = lax.broadcasted_iota(jnp.int32, (C, PW), 0)
    lane = lax.broadcasted_iota(jnp.int32, (C, PW), 1)
    col = lane % N
    lo = lane < N
    strict = row > col
    incl = row >= col
    incl2 = jnp.concatenate([incl, incl], axis=1)
    eye = (row == col).astype(F32)
    tri = (lax.broadcasted_iota(jnp.int32, (C, C), 0) >= lax.broadcasted_iota(jnp.int32, (C, C), 1)).astype(BF16)
    base_f = ((row // 2) == (col // 2)).astype(F32)
    level_f = []
    s = 2
    while s < C:
        level_f.append((((row // (2 * s)) == (col // (2 * s))) & ((row // s) != (col // s))).astype(F32))
        s *= 2
    diag_blocks = ((lax.broadcasted_iota(jnp.int32, (PW, PW), 0) < N)
                   == (lax.broadcasted_iota(jnp.int32, (PW, PW), 1) < N))

    def bd(z):
        zero = jnp.zeros_like(z)
        return jnp.concatenate([jnp.where(lo, z, zero), jnp.where(lo, zero, z)], axis=0)

    def seg_sum(x):
        s_lo = jnp.sum(jnp.where(lo, x, 0.0), axis=-1, keepdims=True)
        s_hi = jnp.sum(jnp.where(lo, 0.0, x), axis=-1, keepdims=True)
        return jnp.where(lo, s_lo, s_hi)

    def chunk(c, _):
        t0 = pl.multiple_of(c * C, C)
        sl = pl.ds(t0, C)
        lw_all = lw_ref[sl, :]
        L_all = _cumsum_rows(tri, lw_all)
        psl = [slice(p * PW, (p + 1) * PW) for p in range(pairs)]
        P = range(pairs)
        r = [r_ref[sl, ps] for ps in psl]
        k = [k_ref[sl, ps] for ps in psl]
        a = [a_ref[sl, ps] for ps in psl]
        vb = [v_ref[sl, ps].astype(BF16) for ps in psl]
        L = [L_all[:, ps] for ps in psl]
        Lc = [x[C - 1:C, :] for x in L]
        km = [k[p] * (1.0 + (a[p] - 1.0) * kap_ref[:, psl[p]]) for p in P]
        kkr = [k[p] * kkp_ref[:, psl[p]] for p in P]
        kk = [kkr[p] * lax.rsqrt(jnp.maximum(seg_sum(kkr[p] * kkr[p]), 1e-24)) for p in P]
        b = [kk[p] * a[p] for p in P]
        e_nl = [jnp.exp(-L[p]) for p in P]
        e_end = [jnp.exp(Lc[p] - L[p]) for p in P]
        lhs = [jnp.concatenate([-kk[p] * jnp.exp(L[p] - lw_all[:, psl[p]]), r[p] * jnp.exp(L[p])],
                               axis=0).astype(BF16) for p in P]
        rhs = [jnp.concatenate([bd((b[p] * e_nl[p]).astype(BF16)), bd((km[p] * e_nl[p]).astype(BF16))],
                               axis=0) for p in P]
        tn_rhs = [jnp.concatenate([(b[p] * e_end[p]).astype(BF16), (km[p] * e_end[p]).astype(BF16)], axis=0)
                  for p in P]
        G = [_dot_nt(lhs[p], rhs[p]) for p in P]
        S0 = [s_scr[p] for p in P]
        SA = [_dot_nt(lhs[p], S0[p].astype(BF16)) for p in P]
        A_ab = [jnp.where(strict, G[p][:C, :PW], 0.0) for p in P]
        A_ak = [jnp.where(strict, G[p][:C, PW:], 0.0).astype(BF16) for p in P]
        P_rbk = [jnp.where(incl2, G[p][C:, :], 0.0).astype(BF16) for p in P]
        vbd = [bd(vb[p]) for p in P]
        W = [SA[p][:C] + _dot(A_ak[p], vbd[p]) for p in P]
        X = [eye + A_ab[p] * base_f for p in P]
        for lvl in level_f:
            Xb = [X[p].astype(BF16) for p in P]
            inner = [_dot((A_ab[p] * lvl).astype(BF16), bd(Xb[p])).astype(BF16) for p in P]
            X = [X[p] + _dot(Xb[p], bd(inner[p])) for p in P]
        ub = [_dot(X[p].astype(BF16), bd(W[p].astype(BF16))).astype(BF16) for p in P]
        Y = [SA[p][C:] + _dot(P_rbk[p], jnp.concatenate([bd(ub[p]), vbd[p]], axis=0)) for p in P]
        for p in P:
            upd = _dot_tn(jnp.concatenate([ub[p], vb[p]], axis=0), tn_rhs[p])
            s_scr[p] = jnp.where(diag_blocks, S0[p] * jnp.exp(Lc[p]) + upd, 0.0)
        for p in P:
            ps = psl[p]
            yc = Y[p] - seg_sum(Y[p]) * (1.0 / N)
            var = seg_sum(yc * yc) * (1.0 / N)
            yn = yc * lax.rsqrt(var + GN_EPS) * lng_ref[:, ps] + lnb_ref[:, ps]
            bonus = seg_sum(r[p] * km[p] * rkp_ref[:, ps]) * v_ref[sl, ps]
            o_ref[sl, ps] = ((yn + bonus) * g_ref[sl, ps]).astype(o_ref.dtype)
        return 0

    lax.fori_loop(0, n_chunks, chunk, 0)


def _wkv(proj, kkp, kap, rkp, lng, lnb, batch, seq, *, hw=2048, tb=256):
    n_proj, T, D = proj.shape
    nt = seq // tb
    slot = lambda c: pl.BlockSpec((None, tb, hw), lambda b, j, t: (c, b * nt + t, j))
    par = pl.BlockSpec((1, hw), lambda b, j, t: (0, j))
    return pl.pallas_call(
        functools.partial(_wkv_kernel, n_chunks=tb // WKV_CHUNK),
        grid=(batch, D // hw, nt),
        in_specs=[slot(c) for c in range(n_proj)] + [par] * 5,
        out_specs=pl.BlockSpec((tb, hw), lambda b, j, t: (b * nt + t, j)),
        out_shape=jax.ShapeDtypeStruct((T, D), BF16),
        scratch_shapes=[pltpu.VMEM((hw // (2 * RW_HEAD), 2 * RW_HEAD, 2 * RW_HEAD), F32)],
        compiler_params=_params("parallel", "parallel", "arbitrary"),
        name="wkv7",
    )(*([proj] * n_proj), kkp.reshape(1, D), kap.reshape(1, D), rkp.reshape(1, D),
      lng.reshape(1, D), lnb.reshape(1, D))


def _stack_low_rank(pairs):
    rank = max(w1.shape[1] for w1, _ in pairs)
    rank += (-rank) % LANES
    l1 = jnp.stack([jnp.pad(w1, ((0, 0), (0, rank - w1.shape[1]))) for w1, _ in pairs])
    l2 = jnp.stack([jnp.pad(w2, ((0, rank - w2.shape[0]), (0, 0))) for _, w2 in pairs])
    return l1.astype(BF16), l2.astype(BF16)


def kernel(x, g_pre_mix, g_post_mix, g_pre_ffn, g_post_ffn, ffn_w1, ffn_w2, da_wq, da_wk, da_wv, da_wo, da_lambda, da_subln, rw_mix, rw_wr, rw_wk, rw_wv, rw_wo, rw_w0, rw_w1, rw_w2, rw_a0, rw_a1, rw_a2, rw_g1, rw_g2, rw_kk, rw_ka, rw_rk, rw_lnx_g, rw_lnx_b):
    B, S, D = x.shape
    depth = g_pre_mix.shape[0]
    h = x.reshape(B * S, D)
    tabs = _rope_tables(S)
    for i in range(depth):
        j = i // N_MIXERS
        if i % N_MIXERS == 0:
            lambda_init = 0.8 - 0.6 * math.exp(-0.3 * i)
            wqkv = jnp.concatenate([da_wq[j], da_wk[j], da_wv[j]], axis=1).astype(BF16)
            qkv = _qkv_proj(h, g_pre_mix[i], wqkv, tabs, S)
            att = _diff_attention(qkv, da_lambda[j], da_subln[j], lambda_init, B, S)
            h = _proj_norm_res(att, da_wo[j].astype(BF16), g_post_mix[i], h)
        else:
            mix = rw_mix[j][jnp.array([0, 2, 3, 1, 4, 5])]
            g_in = g_pre_mix[i][None, :]
            wrkv = jnp.stack([rw_wr[j], rw_wk[j], rw_wv[j]]).astype(BF16)
            l1, l2 = _stack_low_rank([(rw_w1[j], rw_w2[j]), (rw_a1[j], rw_a2[j]), (rw_g1[j], rw_g2[j])])
            lb = jnp.stack([rw_w0[j], rw_a0[j], jnp.zeros_like(rw_a0[j])])[:, None, :]
            proj = _rwkv_in(h, g_in * (1.0 - mix), g_in * mix, wrkv, l1, l2, lb, S)
            y = _wkv(proj, rw_kk[j], rw_ka[j], rw_rk[j].reshape(-1), rw_lnx_g[j], rw_lnx_b[j], B, S)
            h = _proj_norm_res(y, rw_wo[j].astype(BF16), g_post_mix[i], h)
        h = _ffn(h, g_pre_ffn[i], ffn_w1[i].astype(BF16), ffn_w2[i].astype(BF16), g_post_ffn[i])
    return h.reshape(B, S, D)
```

```python
import functools
import math

import jax
import jax.numpy as jnp
from jax import lax
from jax.experimental import pallas as pl
from jax.experimental.pallas import tpu as pltpu

F32 = jnp.float32
BF16 = jnp.bfloat16

CHUNK = 64
DA_HEADS = 8
DA_HEAD_DIM = 128
DA_V_DIM = 2 * DA_HEAD_DIM
ROT_DIM = DA_HEAD_DIM // 4
ROPE_THETA = 500000.0
DA_SUBLN_EPS = 1e-5
RW_HEAD = 64
GN_EPS = 64e-5
EPS = 1e-6
NEG_INF = -1e30
N_MIXERS = 2

VMEM_LIMIT_BYTES = 52 * 1024 * 1024
LANES = 128
WKV_CHUNK = 64


def _params(*sem):
    return pltpu.CompilerParams(dimension_semantics=sem, vmem_limit_bytes=VMEM_LIMIT_BYTES)


def _rms(x, eps):
    return x * lax.rsqrt(jnp.mean(x * x, axis=-1, keepdims=True) + eps)


def _dot(a, b):
    return jnp.dot(a, b, preferred_element_type=F32)


def _dot_nt(a, b):
    return lax.dot_general(a, b, (((1,), (1,)), ((), ())), preferred_element_type=F32)


def _dot_tn(a, b):
    return lax.dot_general(a, b, (((0,), (0,)), ((), ())), preferred_element_type=F32)


def _qkv_kernel(x_ref, g_ref, w_ref, tab_ref, o_ref, a_scr, *, n_sub):
    @pl.when(pl.program_id(1) == 0)
    def _():
        a_scr[...] = (_rms(x_ref[...], EPS) * g_ref[...]).astype(BF16)

    c, s1, s2 = tab_ref[0], tab_ref[1], tab_ref[2]
    half = ROT_DIM // 2
    for n in range(o_ref.shape[1] // n_sub):
        acc = _dot(a_scr[...], w_ref[:, n * n_sub:(n + 1) * n_sub])
        for h in range(n_sub // DA_HEAD_DIM):
            blk = acc[:, h * DA_HEAD_DIM:(h + 1) * DA_HEAD_DIM]
            up = pltpu.roll(blk, DA_HEAD_DIM - half, axis=1)
            dn = pltpu.roll(blk, half, axis=1)
            lanes = slice(n * n_sub + h * DA_HEAD_DIM, n * n_sub + (h + 1) * DA_HEAD_DIM)
            o_ref[:, lanes] = (blk * c + up * s1 + dn * s2).astype(o_ref.dtype)


def _qkv_proj(x, g, w, tabs, seq, *, tm=1024, tn=1024, n_sub=256):
    T, D = x.shape
    N = w.shape[1]
    n_q_tiles = (N // 3) // tn
    tiles_per_seq = seq // tm
    return pl.pallas_call(
        functools.partial(_qkv_kernel, n_sub=n_sub),
        grid=(T // tm, N // tn),
        in_specs=[
            pl.BlockSpec((tm, D), lambda i, j: (i, 0)),
            pl.BlockSpec((1, D), lambda i, j: (0, 0)),
            pl.BlockSpec((D, tn), lambda i, j: (0, j)),
            pl.BlockSpec((None, 3, tm, DA_HEAD_DIM),
                         lambda i, j: (j // n_q_tiles, 0, i % tiles_per_seq, 0)),
        ],
        out_specs=pl.BlockSpec((tm, tn), lambda i, j: (i, j)),
        out_shape=jax.ShapeDtypeStruct((T, N), BF16),
        scratch_shapes=[pltpu.VMEM((tm, D), BF16)],
        compiler_params=_params("parallel", "arbitrary"),
        name="qkv_proj",
    )(x, g.reshape(1, D), w, tabs)


def _rope_tables(seq):
    half = ROT_DIM // 2
    inv_freq = ROPE_THETA ** (-jnp.arange(half, dtype=F32) * (2.0 / ROT_DIM))
    ang = jnp.arange(seq, dtype=F32)[:, None] * inv_freq[None, :]
    cos, sin = jnp.cos(ang), jnp.sin(ang)
    rest = DA_HEAD_DIM - ROT_DIM
    c = jnp.concatenate([cos, cos, jnp.ones((seq, rest), F32)], axis=1)
    s1 = jnp.concatenate([-sin, jnp.zeros((seq, DA_HEAD_DIM - half), F32)], axis=1)
    s2 = jnp.concatenate([jnp.zeros((seq, half), F32), sin, jnp.zeros((seq, rest), F32)], axis=1)
    k_tab = jnp.stack([c, s1, s2])
    v_tab = jnp.stack([jnp.ones_like(c), jnp.zeros_like(c), jnp.zeros_like(c)])
    return jnp.stack([k_tab * (DA_HEAD_DIM ** -0.5), k_tab, v_tab])


def _attn_kernel(lam_ref, q_ref, k_ref, v_ref, g_ref, o_ref, *, blk, lambda_init):
    row = lax.broadcasted_iota(jnp.int32, (blk, blk), 0)
    col = lax.broadcasted_iota(jnp.int32, (blk, blk), 1)
    allowed = (col // CHUNK) <= (row // CHUNK)
    lv = lam_ref[...]
    lam = (jnp.exp(jnp.sum(lv[0:1] * lv[1:2], axis=-1, keepdims=True))
           - jnp.exp(jnp.sum(lv[2:3] * lv[3:4], axis=-1, keepdims=True)) + lambda_init)

    def softmax_pv(m, n):
        hs = slice(m * DA_HEAD_DIM, (m + 1) * DA_HEAD_DIM)
        past = n * blk
        q = q_ref[past:past + blk, hs]
        s_d = jnp.where(allowed, _dot_nt(q, k_ref[past:past + blk, hs]), NEG_INF)
        mx = jnp.max(s_d, axis=-1, keepdims=True)
        if n > 0:
            s_p = _dot_nt(q, k_ref[0:past, hs])
            mx = jnp.maximum(mx, jnp.max(s_p, axis=-1, keepdims=True))
        p_d = jnp.exp(s_d - mx)
        l = jnp.sum(p_d, axis=-1, keepdims=True)
        acc = _dot(p_d.astype(BF16), v_ref[past:past + blk, :])
        if n > 0:
            p_p = jnp.exp(s_p - mx)
            l = l + jnp.sum(p_p, axis=-1, keepdims=True)
            acc = acc + _dot(p_p.astype(BF16), v_ref[0:past, :])
        return acc * (1.0 / l)

    for n in range(k_ref.shape[0] // blk):
        o = softmax_pv(0, n) - lam * softmax_pv(1, n)
        o = _rms(o, DA_SUBLN_EPS) * g_ref[...] * (1.0 - lambda_init)
        o_ref[n * blk:(n + 1) * blk, :] = o.astype(o_ref.dtype)


def _diff_attention(qkv, lam_vecs, subln_g, lambda_init, batch, seq, *, blk=256):
    T = qkv.shape[0]
    return pl.pallas_call(
        functools.partial(_attn_kernel, blk=blk, lambda_init=lambda_init),
        grid=(batch, DA_HEADS),
        in_specs=[
            pl.BlockSpec((4, DA_HEAD_DIM), lambda b, h: (0, 0)),
            pl.BlockSpec((seq, DA_V_DIM), lambda b, h: (b, h)),
            pl.BlockSpec((seq, DA_V_DIM), lambda b, h: (b, DA_HEADS + h)),
            pl.BlockSpec((seq, DA_V_DIM), lambda b, h: (b, 2 * DA_HEADS + h)),
            pl.BlockSpec((1, DA_V_DIM), lambda b, h: (0, 0)),
        ],
        out_specs=pl.BlockSpec((seq, DA_V_DIM), lambda b, h: (b, h)),
        out_shape=jax.ShapeDtypeStruct((T, DA_HEADS * DA_V_DIM), BF16),
        compiler_params=_params("parallel", "parallel"),
        name="diff_attn",
    )(lam_vecs, qkv, qkv, qkv, subln_g.reshape(1, DA_V_DIM))


def _proj_norm_res_kernel(a_ref, w_ref, g_ref, x_ref, o_ref):
    m = _dot(a_ref[...], w_ref[...])
    o_ref[...] = x_ref[...] + _rms(m, EPS) * g_ref[...]


def _proj_norm_res(a, w, g, resid, *, tm=512):
    T, K = a.shape
    N = w.shape[1]
    return pl.pallas_call(
        _proj_norm_res_kernel,
        grid=(T // tm,),
        in_specs=[
            pl.BlockSpec((tm, K), lambda i: (i, 0)),
            pl.BlockSpec((K, N), lambda i: (0, 0)),
            pl.BlockSpec((1, N), lambda i: (0, 0)),
            pl.BlockSpec((tm, N), lambda i: (i, 0)),
        ],
        out_specs=pl.BlockSpec((tm, N), lambda i: (i, 0)),
        out_shape=jax.ShapeDtypeStruct((T, N), F32),
        compiler_params=_params("parallel"),
        name="proj_norm_res",
    )(a, w, g.reshape(1, N), resid)


def _ffn_kernel(x_ref, g1_ref, w1_ref, w2_ref, g2_ref, o_ref, h_scr, acc_scr):
    f = pl.program_id(1)

    @pl.when(f == 0)
    def _():
        h_scr[...] = (_rms(x_ref[...], EPS) * g1_ref[...]).astype(BF16)
        acc_scr[...] = jnp.zeros_like(acc_scr)

    u = jnp.maximum(_dot(h_scr[...], w1_ref[...]), 0.0)
    acc_scr[...] += _dot((u * u).astype(BF16), w2_ref[...])

    @pl.when(f == pl.num_programs(1) - 1)
    def _():
        o_ref[...] = x_ref[...] + _rms(acc_scr[...], EPS) * g2_ref[...]


def _ffn(x, g1, w1, w2, g2, *, tm=512, tf=1024):
    T, D = x.shape
    F = w1.shape[1]
    return pl.pallas_call(
        _ffn_kernel,
        grid=(T // tm, F // tf),
        in_specs=[
            pl.BlockSpec((tm, D), lambda i, f: (i, 0)),
            pl.BlockSpec((1, D), lambda i, f: (0, 0)),
            pl.BlockSpec((D, tf), lambda i, f: (0, f)),
            pl.BlockSpec((tf, D), lambda i, f: (f, 0)),
            pl.BlockSpec((1, D), lambda i, f: (0, 0)),
        ],
        out_specs=pl.BlockSpec((tm, D), lambda i, f: (i, 0)),
        out_shape=jax.ShapeDtypeStruct((T, D), F32),
        scratch_shapes=[pltpu.VMEM((tm, D), BF16), pltpu.VMEM((tm, D), F32)],
        compiler_params=_params("parallel", "arbitrary"),
        name="ffn",
    )(x, g1.reshape(1, D), w1, w2, g2.reshape(1, D))


def _rwkv_in_kernel(x_ref, xp_ref, u_ref, w_ref, wrkv_ref, l1_ref, l2_ref, lb_ref, o_ref, rinv_scr, a_scr,
                    *, tiles_per_seq):
    i = pl.program_id(0)
    s = pl.program_id(1)
    tm = o_ref.shape[0]

    def build(cn, slot):
        xs = x_ref[...] * rinv_scr[...]
        sub = xp_ref.shape[0]
        xp = _rms(xp_ref[sub - 1:sub, :], EPS)
        xp = jnp.where(i % tiles_per_seq == 0, jnp.zeros_like(xp), xp)
        row = lax.broadcasted_iota(jnp.int32, (xs.shape[0], 1), 0)
        prev = jnp.where(row == 0, xp, pltpu.roll(xs, 1, axis=0))
        a_scr[slot] = (xs * u_ref[cn:cn + 1, :] + prev * w_ref[cn:cn + 1, :]).astype(BF16)

    @pl.when(s == 0)
    def _():
        x = x_ref[...]
        rinv_scr[...] = lax.rsqrt(jnp.mean(x * x, axis=-1, keepdims=True) + EPS)
        build(0, 0)

    def dense(slot, builds):
        o_ref[...] = _dot(a_scr[slot], wrkv_ref[...])
        for cn, dst in builds:
            build(cn, dst)

    def sigmoid(z):
        return 0.5 * jnp.tanh(0.5 * z) + 0.5

    def low_rank(slot, pre, post):
        halves = [slice(h * (tm // 2), (h + 1) * (tm // 2)) for h in range(2)]
        t = [_dot(a_scr[slot, rows, :], l1_ref[...]) for rows in halves]
        t = [pre(v).astype(BF16) for v in t]
        z = [_dot(v, l2_ref[...]) for v in t]
        for rows, v in zip(halves, z):
            o_ref[rows, :] = post(v)

    pl.when(s == 0)(lambda: dense(0, [(3, 1), (1, 2)]))
    pl.when(s == 1)(lambda: low_rank(1, jnp.tanh, lambda z: -math.exp(-0.5) * sigmoid(z + lb_ref[...])))
    pl.when(s == 2)(lambda: dense(2, [(4, 0), (2, 1)]))
    pl.when(s == 3)(lambda: low_rank(0, lambda t: t, lambda z: sigmoid(z + lb_ref[...])))
    pl.when(s == 4)(lambda: dense(1, [(5, 2)]))
    pl.when(s == 5)(lambda: low_rank(2, sigmoid, lambda z: z))


def _rwkv_in(x, u, w, wrkv, l1, l2, lb, seq, *, tm=512, sub=8):
    T, D = x.shape
    n_proj = u.shape[0]
    n_dense = wrkv.shape[0]
    R = l1.shape[2]
    per = tm // sub
    assert n_proj == 2 * n_dense
    low = lambda i, s: (s // 2, 0, 0)
    return pl.pallas_call(
        functools.partial(_rwkv_in_kernel, tiles_per_seq=seq // tm),
        grid=(T // tm, n_proj),
        in_specs=[
            pl.BlockSpec((tm, D), lambda i, s: (i, 0)),
            pl.BlockSpec((sub, D), lambda i, s: (jnp.maximum(i * per - 1, 0), 0)),
            pl.BlockSpec((n_proj, D), lambda i, s: (0, 0)),
            pl.BlockSpec((n_proj, D), lambda i, s: (0, 0)),
            pl.BlockSpec((None, D, D), lambda i, s: (((s + 1) // 2) % n_dense, 0, 0)),
            pl.BlockSpec((None, D, R), low),
            pl.BlockSpec((None, R, D), low),
            pl.BlockSpec((None, 1, D), low),
        ],
        out_specs=pl.BlockSpec((None, tm, D), lambda i, s: (s // 2 + n_dense * (s % 2), i, 0)),
        out_shape=jax.ShapeDtypeStruct((n_proj, T, D), F32),
        scratch_shapes=[pltpu.VMEM((tm, 1), F32), pltpu.VMEM((3, tm, D), BF16)],
        compiler_params=_params("parallel", "arbitrary"),
        name="rwkv_in",
    )(x, x, u, w, wrkv, l1, l2, lb)


def _cumsum_rows(tri, x):
    hi = x.astype(BF16)
    r1 = x - hi.astype(F32)
    mid = r1.astype(BF16)
    lo = (r1 - mid.astype(F32)).astype(BF16)
    return _dot(tri, hi) + _dot(tri, mid) + _dot(tri, lo)


def _wkv_kernel(r_ref, k_ref, v_ref, lw_ref, a_ref, g_ref, kkp_ref, kap_ref, rkp_ref, lng_ref, lnb_ref,
                o_ref, s_scr, *, n_chunks):
    C = WKV_CHUNK
    N = RW_HEAD
    PW = 2 * N
    pairs = s_scr.shape[0]

    @pl.when(pl.program_id(2) == 0)
    def _():
        s_scr[...] = jnp.zeros_like(s_scr)

    row = lax.broadcasted_iota(jnp.int32, (C, PW), 0)
    lane = lax.broadcasted_iota(jnp.int32, (C, PW), 1)
    col = lane % N
    lo = lane < N
    strict = row > col
    incl = row >= col
    incl2 = jnp.concatenate([incl, incl], axis=1)
    eye = (row == col).astype(F32)
    tri = (lax.broadcasted_iota(jnp.int32, (C, C), 0) >= lax.broadcasted_iota(jnp.int32, (C, C), 1)).astype(BF16)
    base_f = ((row // 2) == (col // 2)).astype(F32)
    level_f = []
    s = 2
    while s < C:
        level_f.append((((row // (2 * s)) == (col // (2 * s))) & ((row // s) != (col // s))).astype(F32))
        s *= 2
    diag_blocks = ((lax.broadcasted_iota(jnp.int32, (PW, PW), 0) < N)
                   == (lax.broadcasted_iota(jnp.int32, (PW, PW), 1) < N))

    def bd(z):
        zero = jnp.zeros_like(z)
        return jnp.concatenate([jnp.where(lo, z, zero), jnp.where(lo, zero, z)], axis=0)

    def seg_sum(x):
        s_lo = jnp.sum(jnp.where(lo, x, 0.0), axis=-1, keepdims=True)
        s_hi = jnp.sum(jnp.where(lo, 0.0, x), axis=-1, keepdims=True)
        return jnp.where(lo, s_lo, s_hi)

    def chunk(c, _):
        t0 = pl.multiple_of(c * C, C)
        sl = pl.ds(t0, C)
        lw_all = lw_ref[sl, :]
        L_all = _cumsum_rows(tri, lw_all)
        psl = [slice(p * PW, (p + 1) * PW) for p in range(pairs)]
        P = range(pairs)
        r = [r_ref[sl, ps] for ps in psl]
        k = [k_ref[sl, ps] for ps in psl]
        a = [a_ref[sl, ps] for ps in psl]
        vb = [v_ref[sl, ps].astype(BF16) for ps in psl]
        L = [L_all[:, ps] for ps in psl]
        Lc = [x[C - 1:C, :] for x in L]
        km = [k[p] * (1.0 + (a[p] - 1.0) * kap_ref[:, psl[p]]) for p in P]
        kkr = [k[p] * kkp_ref[:, psl[p]] for p in P]
        kk = [kkr[p] * lax.rsqrt(jnp.maximum(seg_sum(kkr[p] * kkr[p]), 1e-24)) for p in P]
        b = [kk[p] * a[p] for p in P]
        e_nl = [jnp.exp(-L[p]) for p in P]
        e_end = [jnp.exp(Lc[p] - L[p]) for p in P]
        lhs = [jnp.concatenate([-kk[p] * jnp.exp(L[p] - lw_all[:, psl[p]]), r[p] * jnp.exp(L[p])],
                               axis=0).astype(BF16) for p in P]
        rhs = [jnp.concatenate([bd((b[p] * e_nl[p]).astype(BF16)), bd((km[p] * e_nl[p]).astype(BF16))],
                               axis=0) for p in P]
        tn_rhs = [jnp.concatenate([(b[p] * e_end[p]).astype(BF16), (km[p] * e_end[p]).astype(BF16)], axis=0)
                  for p in P]
        G = [_dot_nt(lhs[p], rhs[p]) for p in P]
        S0 = [s_scr[p] for p in P]
        SA = [_dot_nt(lhs[p], S0[p].astype(BF16)) for p in P]
        A_ab = [jnp.where(strict, G[p][:C, :PW], 0.0) for p in P]
        A_ak = [jnp.where(strict, G[p][:C, PW:], 0.0).astype(BF16) for p in P]
        P_rbk = [jnp.where(incl2, G[p][C:, :], 0.0).astype(BF16) for p in P]
        vbd = [bd(vb[p]) for p in P]
        W = [SA[p][:C] + _dot(A_ak[p], vbd[p]) for p in P]
        X = [eye + A_ab[p] * base_f for p in P]
        for lvl in level_f:
            Xb = [X[p].astype(BF16) for p in P]
            inner = [_dot((A_ab[p] * lvl).astype(BF16), bd(Xb[p])).astype(BF16) for p in P]
            X = [X[p] + _dot(Xb[p], bd(inner[p])) for p in P]
        ub = [_dot(X[p].astype(BF16), bd(W[p].astype(BF16))).astype(BF16) for p in P]
        Y = [SA[p][C:] + _dot(P_rbk[p], jnp.concatenate([bd(ub[p]), vbd[p]], axis=0)) for p in P]
        for p in P:
            upd = _dot_tn(jnp.concatenate([ub[p], vb[p]], axis=0), tn_rhs[p])
            s_scr[p] = jnp.where(diag_blocks, S0[p] * jnp.exp(Lc[p]) + upd, 0.0)
        for p in P:
            ps = psl[p]
            yc = Y[p] - seg_sum(Y[p]) * (1.0 / N)
            var = seg_sum(yc * yc) * (1.0 / N)
            yn = yc * lax.rsqrt(var + GN_EPS) * lng_ref[:, ps] + lnb_ref[:, ps]
            bonus = seg_sum(r[p] * km[p] * rkp_ref[:, ps]) * v_ref[sl, ps]
            o_ref[sl, ps] = ((yn + bonus) * g_ref[sl, ps]).astype(o_ref.dtype)
        return 0

    lax.fori_loop(0, n_chunks, chunk, 0)


def _wkv(proj, kkp, kap, rkp, lng, lnb, batch, seq, *, hw=2048, tb=256):
    n_proj, T, D = proj.shape
    nt = seq // tb
    slot = lambda c: pl.BlockSpec((None, tb, hw), lambda b, j, t: (c, b * nt + t, j))
    par = pl.BlockSpec((1, hw), lambda b, j, t: (0, j))
    return pl.pallas_call(
        functools.partial(_wkv_kernel, n_chunks=tb // WKV_CHUNK),
        grid=(batch, D // hw, nt),
        in_specs=[slot(c) for c in range(n_proj)] + [par] * 5,
        out_specs=pl.BlockSpec((tb, hw), lambda b, j, t: (b * nt + t, j)),
        out_shape=jax.ShapeDtypeStruct((T, D), BF16),
        scratch_shapes=[pltpu.VMEM((hw // (2 * RW_HEAD), 2 * RW_HEAD, 2 * RW_HEAD), F32)],
        compiler_params=_params("parallel", "parallel", "arbitrary"),
        name="wkv7",
    )(*([proj] * n_proj), kkp.reshape(1, D), kap.reshape(1, D), rkp.reshape(1, D),
      lng.reshape(1, D), lnb.reshape(1, D))


def _cast_kernel(w_ref, o_ref):
    o_ref[...] = w_ref[...].astype(o_ref.dtype)


CAST_BLOCK_BYTES = 8 * 1024 * 1024


def _layer_bf16(w, layer):
    _, R, C = w.shape
    br = min(R, max(16, CAST_BLOCK_BYTES // (4 * C)))
    return pl.pallas_call(
        _cast_kernel,
        grid=(R // br,),
        in_specs=[pl.BlockSpec((None, br, C), lambda r: (layer, r, 0))],
        out_specs=pl.BlockSpec((br, C), lambda r: (r, 0)),
        out_shape=jax.ShapeDtypeStruct((R, C), BF16),
        compiler_params=_params("parallel"),
        name="cast_bf16",
    )(w)


def _stack_low_rank(pairs):
    rank = max(w1.shape[1] for w1, _ in pairs)
    rank += (-rank) % LANES
    l1 = jnp.stack([jnp.pad(w1, ((0, 0), (0, rank - w1.shape[1]))) for w1, _ in pairs])
    l2 = jnp.stack([jnp.pad(w2, ((0, rank - w2.shape[0]), (0, 0))) for _, w2 in pairs])
    return l1.astype(BF16), l2.astype(BF16)


def kernel(x, g_pre_mix, g_post_mix, g_pre_ffn, g_post_ffn, ffn_w1, ffn_w2, da_wq, da_wk, da_wv, da_wo, da_lambda, da_subln, rw_mix, rw_wr, rw_wk, rw_wv, rw_wo, rw_w0, rw_w1, rw_w2, rw_a0, rw_a1, rw_a2, rw_g1, rw_g2, rw_kk, rw_ka, rw_rk, rw_lnx_g, rw_lnx_b):
    B, S, D = x.shape
    depth = g_pre_mix.shape[0]
    h = x.reshape(B * S, D)
    tabs = _rope_tables(S)
    for i in range(depth):
        j = i // N_MIXERS
        if i % N_MIXERS == 0:
            lambda_init = 0.8 - 0.6 * math.exp(-0.3 * i)
            wqkv = jnp.concatenate([_layer_bf16(w, j) for w in (da_wq, da_wk, da_wv)], axis=1)
            qkv = _qkv_proj(h, g_pre_mix[i], wqkv, tabs, S)
            att = _diff_attention(qkv, da_lambda[j], da_subln[j], lambda_init, B, S)
            h = _proj_norm_res(att, _layer_bf16(da_wo, j), g_post_mix[i], h)
        else:
            mix = rw_mix[j][jnp.array([0, 2, 3, 1, 4, 5])]
            g_in = g_pre_mix[i][None, :]
            wrkv = jnp.stack([_layer_bf16(w, j) for w in (rw_wr, rw_wk, rw_wv)])
            l1, l2 = _stack_low_rank([(rw_w1[j], rw_w2[j]), (rw_a1[j], rw_a2[j]), (rw_g1[j], rw_g2[j])])
            lb = jnp.stack([rw_w0[j], rw_a0[j], jnp.zeros_like(rw_a0[j])])[:, None, :]
            proj = _rwkv_in(h, g_in * (1.0 - mix), g_in * mix, wrkv, l1, l2, lb, S)
            y = _wkv(proj, rw_kk[j], rw_ka[j], rw_rk[j].reshape(-1), rw_lnx_g[j], rw_lnx_b[j], B, S)
            h = _proj_norm_res(y, _layer_bf16(rw_wo, j), g_post_mix[i], h)
        h = _ffn(h, g_pre_ffn[i], _layer_bf16(ffn_w1, i), _layer_bf16(ffn_w2, i), g_post_ffn[i])
    return h.reshape(B, S, D)
```

```python
import functools
import math

import jax
import jax.numpy as jnp
from jax import lax
from jax.experimental import pallas as pl
from jax.experimental.pallas import tpu as pltpu

F32 = jnp.float32
BF16 = jnp.bfloat16

CHUNK = 64
DA_HEADS = 8
DA_HEAD_DIM = 128
DA_V_DIM = 2 * DA_HEAD_DIM
ROT_DIM = DA_HEAD_DIM // 4
ROPE_THETA = 500000.0
DA_SUBLN_EPS = 1e-5
RW_HEAD = 64
GN_EPS = 64e-5
EPS = 1e-6
NEG_INF = -1e30
N_MIXERS = 2

VMEM_LIMIT_BYTES = 52 * 1024 * 1024
LANES = 128
WKV_CHUNK = 64


def _params(*sem):
    return pltpu.CompilerParams(dimension_semantics=sem, vmem_limit_bytes=VMEM_LIMIT_BYTES)


def _rms(x, eps):
    return x * lax.rsqrt(jnp.mean(x * x, axis=-1, keepdims=True) + eps)


def _dot(a, b):
    return jnp.dot(a, b, preferred_element_type=F32)


def _dot_nt(a, b):
    return lax.dot_general(a, b, (((1,), (1,)), ((), ())), preferred_element_type=F32)


def _dot_tn(a, b):
    return lax.dot_general(a, b, (((0,), (0,)), ((), ())), preferred_element_type=F32)


def _qkv_kernel(x_ref, g_ref, w_ref, tab_ref, o_ref, a_scr, *, n_sub):
    @pl.when(pl.program_id(1) == 0)
    def _():
        a_scr[...] = (_rms(x_ref[...], EPS) * g_ref[...]).astype(BF16)

    c, s1, s2 = tab_ref[0], tab_ref[1], tab_ref[2]
    half = ROT_DIM // 2
    for n in range(o_ref.shape[1] // n_sub):
        acc = _dot(a_scr[...], w_ref[:, n * n_sub:(n + 1) * n_sub])
        for h in range(n_sub // DA_HEAD_DIM):
            blk = acc[:, h * DA_HEAD_DIM:(h + 1) * DA_HEAD_DIM]
            up = pltpu.roll(blk, DA_HEAD_DIM - half, axis=1)
            dn = pltpu.roll(blk, half, axis=1)
            lanes = slice(n * n_sub + h * DA_HEAD_DIM, n * n_sub + (h + 1) * DA_HEAD_DIM)
            o_ref[:, lanes] = (blk * c + up * s1 + dn * s2).astype(o_ref.dtype)


def _qkv_proj(x, g, w, tabs, seq, *, tm=1024, tn=1024, n_sub=256):
    T, D = x.shape
    N = w.shape[1]
    n_q_tiles = (N // 3) // tn
    tiles_per_seq = seq // tm
    return pl.pallas_call(
        functools.partial(_qkv_kernel, n_sub=n_sub),
        grid=(T // tm, N // tn),
        in_specs=[
            pl.BlockSpec((tm, D), lambda i, j: (i, 0)),
            pl.BlockSpec((1, D), lambda i, j: (0, 0)),
            pl.BlockSpec((D, tn), lambda i, j: (0, j)),
            pl.BlockSpec((None, 3, tm, DA_HEAD_DIM),
                         lambda i, j: (j // n_q_tiles, 0, i % tiles_per_seq, 0)),
        ],
        out_specs=pl.BlockSpec((tm, tn), lambda i, j: (i, j)),
        out_shape=jax.ShapeDtypeStruct((T, N), BF16),
        scratch_shapes=[pltpu.VMEM((tm, D), BF16)],
        compiler_params=_params("parallel", "arbitrary"),
        name="qkv_proj",
    )(x, g.reshape(1, D), w, tabs)


def _rope_tables(seq):
    half = ROT_DIM // 2
    inv_freq = ROPE_THETA ** (-jnp.arange(half, dtype=F32) * (2.0 / ROT_DIM))
    ang = jnp.arange(seq, dtype=F32)[:, None] * inv_freq[None, :]
    cos, sin = jnp.cos(ang), jnp.sin(ang)
    rest = DA_HEAD_DIM - ROT_DIM
    c = jnp.concatenate([cos, cos, jnp.ones((seq, rest), F32)], axis=1)
    s1 = jnp.concatenate([-sin, jnp.zeros((seq, DA_HEAD_DIM - half), F32)], axis=1)
    s2 = jnp.concatenate([jnp.zeros((seq, half), F32), sin, jnp.zeros((seq, rest), F32)], axis=1)
    k_tab = jnp.stack([c, s1, s2])
    v_tab = jnp.stack([jnp.ones_like(c), jnp.zeros_like(c), jnp.zeros_like(c)])
    return jnp.stack([k_tab * (DA_HEAD_DIM ** -0.5), k_tab, v_tab])


def _attn_kernel(lam_ref, q_ref, k_ref, v_ref, g_ref, o_ref, *, blk, lambda_init):
    row = lax.broadcasted_iota(jnp.int32, (blk, blk), 0)
    col = lax.broadcasted_iota(jnp.int32, (blk, blk), 1)
    allowed = (col // CHUNK) <= (row // CHUNK)
    lv = lam_ref[...]
    lam = (jnp.exp(jnp.sum(lv[0:1] * lv[1:2], axis=-1, keepdims=True))
           - jnp.exp(jnp.sum(lv[2:3] * lv[3:4], axis=-1, keepdims=True)) + lambda_init)

    def softmax_pv(m, n):
        hs = slice(m * DA_HEAD_DIM, (m + 1) * DA_HEAD_DIM)
        past = n * blk
        q = q_ref[past:past + blk, hs]
        s_d = jnp.where(allowed, _dot_nt(q, k_ref[past:past + blk, hs]), NEG_INF)
        mx = jnp.max(s_d, axis=-1, keepdims=True)
        if n > 0:
            s_p = _dot_nt(q, k_ref[0:past, hs])
            mx = jnp.maximum(mx, jnp.max(s_p, axis=-1, keepdims=True))
        p_d = jnp.exp(s_d - mx)
        l = jnp.sum(p_d, axis=-1, keepdims=True)
        acc = _dot(p_d.astype(BF16), v_ref[past:past + blk, :])
        if n > 0:
            p_p = jnp.exp(s_p - mx)
            l = l + jnp.sum(p_p, axis=-1, keepdims=True)
            acc = acc + _dot(p_p.astype(BF16), v_ref[0:past, :])
        return acc * (1.0 / l)

    for n in range(k_ref.shape[0] // blk):
        o = softmax_pv(0, n) - lam * softmax_pv(1, n)
        o = _rms(o, DA_SUBLN_EPS) * g_ref[...] * (1.0 - lambda_init)
        o_ref[n * blk:(n + 1) * blk, :] = o.astype(o_ref.dtype)


def _diff_attention(qkv, lam_vecs, subln_g, lambda_init, batch, seq, *, blk=256):
    T = qkv.shape[0]
    return pl.pallas_call(
        functools.partial(_attn_kernel, blk=blk, lambda_init=lambda_init),
        grid=(batch, DA_HEADS),
        in_specs=[
            pl.BlockSpec((4, DA_HEAD_DIM), lambda b, h: (0, 0)),
            pl.BlockSpec((seq, DA_V_DIM), lambda b, h: (b, h)),
            pl.BlockSpec((seq, DA_V_DIM), lambda b, h: (b, DA_HEADS + h)),
            pl.BlockSpec((seq, DA_V_DIM), lambda b, h: (b, 2 * DA_HEADS + h)),
            pl.BlockSpec((1, DA_V_DIM), lambda b, h: (0, 0)),
        ],
        out_specs=pl.BlockSpec((seq, DA_V_DIM), lambda b, h: (b, h)),
        out_shape=jax.ShapeDtypeStruct((T, DA_HEADS * DA_V_DIM), BF16),
        compiler_params=_params("parallel", "parallel"),
        name="diff_attn",
    )(lam_vecs, qkv, qkv, qkv, subln_g.reshape(1, DA_V_DIM))


def _proj_norm_res_kernel(a_ref, w_ref, g_ref, x_ref, o_ref):
    m = _dot(a_ref[...], w_ref[...])
    o_ref[...] = x_ref[...] + _rms(m, EPS) * g_ref[...]


def _proj_norm_res(a, w, g, resid, *, tm=512):
    T, K = a.shape
    N = w.shape[1]
    return pl.pallas_call(
        _proj_norm_res_kernel,
        grid=(T // tm,),
        in_specs=[
            pl.BlockSpec((tm, K), lambda i: (i, 0)),
            pl.BlockSpec((K, N), lambda i: (0, 0)),
            pl.BlockSpec((1, N), lambda i: (0, 0)),
            pl.BlockSpec((tm, N), lambda i: (i, 0)),
        ],
        out_specs=pl.BlockSpec((tm, N), lambda i: (i, 0)),
        out_shape=jax.ShapeDtypeStruct((T, N), F32),
        compiler_params=_params("parallel"),
        name="proj_norm_res",
    )(a, w, g.reshape(1, N), resid)


def _ffn_kernel(x_ref, g1_ref, w1_ref, w2_ref, g2_ref, o_ref, h_scr, acc_scr):
    f = pl.program_id(1)

    @pl.when(f == 0)
    def _():
        h_scr[...] = (_rms(x_ref[...], EPS) * g1_ref[...]).astype(BF16)
        acc_scr[...] = jnp.zeros_like(acc_scr)

    u = jnp.maximum(_dot(h_scr[...], w1_ref[...]), 0.0)
    acc_scr[...] += _dot((u * u).astype(BF16), w2_ref[...])

    @pl.when(f == pl.num_programs(1) - 1)
    def _():
        o_ref[...] = x_ref[...] + _rms(acc_scr[...], EPS) * g2_ref[...]


def _ffn(x, g1, w1, w2, g2, *, tm=512, tf=1024):
    T, D = x.shape
    F = w1.shape[1]
    return pl.pallas_call(
        _ffn_kernel,
        grid=(T // tm, F // tf),
        in_specs=[
            pl.BlockSpec((tm, D), lambda i, f: (i, 0)),
            pl.BlockSpec((1, D), lambda i, f: (0, 0)),
            pl.BlockSpec((D, tf), lambda i, f: (0, f)),
            pl.BlockSpec((tf, D), lambda i, f: (f, 0)),
            pl.BlockSpec((1, D), lambda i, f: (0, 0)),
        ],
        out_specs=pl.BlockSpec((tm, D), lambda i, f: (i, 0)),
        out_shape=jax.ShapeDtypeStruct((T, D), F32),
        scratch_shapes=[pltpu.VMEM((tm, D), BF16), pltpu.VMEM((tm, D), F32)],
        compiler_params=_params("parallel", "arbitrary"),
        name="ffn",
    )(x, g1.reshape(1, D), w1, w2, g2.reshape(1, D))


def _rwkv_pre_kernel(x_ref, xp_ref, u_ref, w_ref, l1_ref, l2_ref, lb_ref, mixed_ref, low_ref, *, tiles_per_seq):
    i = pl.program_id(0)
    n_dense = mixed_ref.shape[0]
    xs = _rms(x_ref[...], EPS)
    sub = xp_ref.shape[0]
    xp = _rms(xp_ref[sub - 1:sub, :], EPS)
    xp = jnp.where(i % tiles_per_seq == 0, jnp.zeros_like(xp), xp)
    row = lax.broadcasted_iota(jnp.int32, (xs.shape[0], 1), 0)
    prev = jnp.where(row == 0, xp, pltpu.roll(xs, 1, axis=0))

    def mixed(c):
        return (xs * u_ref[c:c + 1, :] + prev * w_ref[c:c + 1, :]).astype(BF16)

    def sigmoid(z):
        return 0.5 * jnp.tanh(0.5 * z) + 0.5

    for c in range(n_dense):
        mixed_ref[c] = mixed(c)
    pre = (jnp.tanh, lambda t: t, sigmoid)
    post = (lambda z: -math.exp(-0.5) * sigmoid(z + lb_ref[0]), lambda z: sigmoid(z + lb_ref[1]), lambda z: z)
    t = [_dot(mixed(n_dense + l), l1_ref[l]) for l in range(3)]
    t = [pre[l](t[l]).astype(BF16) for l in range(3)]
    z = [_dot(t[l], l2_ref[l]) for l in range(3)]
    for l in range(3):
        low_ref[l] = post[l](z[l])


def _rwkv_pre(x, u, w, l1, l2, lb, seq, *, tm=256, sub=8):
    T, D = x.shape
    n_proj = u.shape[0]
    n_low, _, R = l1.shape
    n_dense = n_proj - n_low
    per = tm // sub
    whole = lambda shape: pl.BlockSpec(shape, lambda i: (0,) * len(shape))
    return pl.pallas_call(
        functools.partial(_rwkv_pre_kernel, tiles_per_seq=seq // tm),
        grid=(T // tm,),
        in_specs=[
            pl.BlockSpec((tm, D), lambda i: (i, 0)),
            pl.BlockSpec((sub, D), lambda i: (jnp.maximum(i * per - 1, 0), 0)),
            whole((n_proj, D)), whole((n_proj, D)),
            whole((n_low, D, R)), whole((n_low, R, D)), whole((n_low, 1, D)),
        ],
        out_specs=[pl.BlockSpec((n_dense, tm, D), lambda i: (0, i, 0)),
                   pl.BlockSpec((n_low, tm, D), lambda i: (0, i, 0))],
        out_shape=[jax.ShapeDtypeStruct((n_dense, T, D), BF16), jax.ShapeDtypeStruct((n_low, T, D), F32)],
        compiler_params=_params("parallel"),
        name="rwkv_pre",
    )(x, x, u, w, l1, l2, lb)


def _bmm_kernel(a_ref, w_ref, o_ref):
    o_ref[...] = _dot(a_ref[...], w_ref[...])


def _bmm(a, w, *, tm=1024, tn=1024):
    P, K, N = w.shape
    T = a.shape[1]
    return pl.pallas_call(
        _bmm_kernel,
        grid=(P, T // tm, N // tn),
        in_specs=[
            pl.BlockSpec((None, tm, K), lambda p, i, j: (p, i, 0)),
            pl.BlockSpec((None, K, tn), lambda p, i, j: (p, 0, j)),
        ],
        out_specs=pl.BlockSpec((None, tm, tn), lambda p, i, j: (p, i, j)),
        out_shape=jax.ShapeDtypeStruct((P, T, N), F32),
        compiler_params=_params("parallel", "parallel", "arbitrary"),
        name="rkv_proj",
    )(a, w)


def _cumsum_rows(tri, x):
    hi = x.astype(BF16)
    r1 = x - hi.astype(F32)
    mid = r1.astype(BF16)
    lo = (r1 - mid.astype(F32)).astype(BF16)
    return _dot(tri, hi) + _dot(tri, mid) + _dot(tri, lo)


def _wkv_kernel(r_ref, k_ref, v_ref, lw_ref, a_ref, g_ref, kkp_ref, kap_ref, rkp_ref, lng_ref, lnb_ref,
                o_ref, s_scr, *, n_chunks):
    C = WKV_CHUNK
    N = RW_HEAD
    PW = 2 * N
    pairs = s_scr.shape[0]

    @pl.when(pl.program_id(2) == 0)
    def _():
        s_scr[...] = jnp.zeros_like(s_scr)

    row = lax.broadcasted_iota(jnp.int32, (C, PW), 0)
    lane = lax.broadcasted_iota(jnp.int32, (C, PW), 1)
    col = lane % N
    lo = lane < N
    strict = row > col
    incl = row >= col
    incl2 = jnp.concatenate([incl, incl], axis=1)
    eye = (row == col).astype(F32)
    tri = (lax.broadcasted_iota(jnp.int32, (C, C), 0) >= lax.broadcasted_iota(jnp.int32, (C, C), 1)).astype(BF16)
    base_f = ((row // 2) == (col // 2)).astype(F32)
    level_f = []
    s = 2
    while s < C:
        level_f.append((((row // (2 * s)) == (col // (2 * s))) & ((row // s) != (col // s))).astype(F32))
        s *= 2
    diag_blocks = ((lax.broadcasted_iota(jnp.int32, (PW, PW), 0) < N)
                   == (lax.broadcasted_iota(jnp.int32, (PW, PW), 1) < N))

    def bd(z):
        zero = jnp.zeros_like(z)
        return jnp.concatenate([jnp.where(lo, z, zero), jnp.where(lo, zero, z)], axis=0)

    def seg_sum(x):
        s_lo = jnp.sum(jnp.where(lo, x, 0.0), axis=-1, keepdims=True)
        s_hi = jnp.sum(jnp.where(lo, 0.0, x), axis=-1, keepdims=True)
        return jnp.where(lo, s_lo, s_hi)

    def chunk(c, _):
        t0 = pl.multiple_of(c * C, C)
        sl = pl.ds(t0, C)
        lw_all = lw_ref[sl, :]
        L_all = _cumsum_rows(tri, lw_all)
        psl = [slice(p * PW, (p + 1) * PW) for p in range(pairs)]
        P = range(pairs)
        r = [r_ref[sl, ps] for ps in psl]
        k = [k_ref[sl, ps] for ps in psl]
        a = [a_ref[sl, ps] for ps in psl]
        vb = [v_ref[sl, ps].astype(BF16) for ps in psl]
        L = [L_all[:, ps] for ps in psl]
        Lc = [x[C - 1:C, :] for x in L]
        km = [k[p] * (1.0 + (a[p] - 1.0) * kap_ref[:, psl[p]]) for p in P]
        kkr = [k[p] * kkp_ref[:, psl[p]] for p in P]
        kk = [kkr[p] * lax.rsqrt(jnp.maximum(seg_sum(kkr[p] * kkr[p]), 1e-24)) for p in P]
        b = [kk[p] * a[p] for p in P]
        e_nl = [jnp.exp(-L[p]) for p in P]
        e_end = [jnp.exp(Lc[p] - L[p]) for p in P]
        lhs = [jnp.concatenate([-kk[p] * jnp.exp(L[p] - lw_all[:, psl[p]]), r[p] * jnp.exp(L[p])],
                               axis=0).astype(BF16) for p in P]
        rhs = [jnp.concatenate([bd((b[p] * e_nl[p]).astype(BF16)), bd((km[p] * e_nl[p]).astype(BF16))],
                               axis=0) for p in P]
        tn_rhs = [jnp.concatenate([(b[p] * e_end[p]).astype(BF16), (km[p] * e_end[p]).astype(BF16)], axis=0)
                  for p in P]
        G = [_dot_nt(lhs[p], rhs[p]) for p in P]
        S0 = [s_scr[p] for p in P]
        SA = [_dot_nt(lhs[p], S0[p].astype(BF16)) for p in P]
        A_ab = [jnp.where(strict, G[p][:C, :PW], 0.0) for p in P]
        A_ak = [jnp.where(strict, G[p][:C, PW:], 0.0).astype(BF16) for p in P]
        P_rbk = [jnp.where(incl2, G[p][C:, :], 0.0).astype(BF16) for p in P]
        vbd = [bd(vb[p]) for p in P]
        W = [SA[p][:C] + _dot(A_ak[p], vbd[p]) for p in P]
        X = [eye + A_ab[p] * base_f for p in P]
        for lvl in level_f:
            Xb = [X[p].astype(BF16) for p in P]
            inner = [_dot((A_ab[p] * lvl).astype(BF16), bd(Xb[p])).astype(BF16) for p in P]
            X = [X[p] + _dot(Xb[p], bd(inner[p])) for p in P]
        ub = [_dot(X[p].astype(BF16), bd(W[p].astype(BF16))).astype(BF16) for p in P]
        Y = [SA[p][C:] + _dot(P_rbk[p], jnp.concatenate([bd(ub[p]), vbd[p]], axis=0)) for p in P]
        for p in P:
            upd = _dot_tn(jnp.concatenate([ub[p], vb[p]], axis=0), tn_rhs[p])
            s_scr[p] = jnp.where(diag_blocks, S0[p] * jnp.exp(Lc[p]) + upd, 0.0)
        for p in P:
            ps = psl[p]
            yc = Y[p] - seg_sum(Y[p]) * (1.0 / N)
            var = seg_sum(yc * yc) * (1.0 / N)
            yn = yc * lax.rsqrt(var + GN_EPS) * lng_ref[:, ps] + lnb_ref[:, ps]
            bonus = seg_sum(r[p] * km[p] * rkp_ref[:, ps]) * v_ref[sl, ps]
            o_ref[sl, ps] = ((yn + bonus) * g_ref[sl, ps]).astype(o_ref.dtype)
        return 0

    lax.fori_loop(0, n_chunks, chunk, 0)


def _wkv(rkv, low, kkp, kap, rkp, lng, lnb, batch, seq, *, hw=2048, tb=256):
    _, T, D = rkv.shape
    nt = seq // tb
    slot = lambda c: pl.BlockSpec((None, tb, hw), lambda b, j, t: (c, b * nt + t, j))
    par = pl.BlockSpec((1, hw), lambda b, j, t: (0, j))
    return pl.pallas_call(
        functools.partial(_wkv_kernel, n_chunks=tb // WKV_CHUNK),
        grid=(batch, D // hw, nt),
        in_specs=[slot(c) for c in range(3)] * 2 + [par] * 5,
        out_specs=pl.BlockSpec((tb, hw), lambda b, j, t: (b * nt + t, j)),
        out_shape=jax.ShapeDtypeStruct((T, D), BF16),
        scratch_shapes=[pltpu.VMEM((hw // (2 * RW_HEAD), 2 * RW_HEAD, 2 * RW_HEAD), F32)],
        compiler_params=_params("parallel", "parallel", "arbitrary"),
        name="wkv7",
    )(rkv, rkv, rkv, low, low, low, kkp.reshape(1, D), kap.reshape(1, D), rkp.reshape(1, D),
      lng.reshape(1, D), lnb.reshape(1, D))


def _cast_kernel(w_ref, o_ref):
    o_ref[...] = w_ref[...].astype(o_ref.dtype)


CAST_BLOCK_BYTES = 8 * 1024 * 1024


def _layer_bf16(w, layer):
    _, R, C = w.shape
    br = min(R, max(16, CAST_BLOCK_BYTES // (4 * C)))
    return pl.pallas_call(
        _cast_kernel,
        grid=(R // br,),
        in_specs=[pl.BlockSpec((None, br, C), lambda r: (layer, r, 0))],
        out_specs=pl.BlockSpec((br, C), lambda r: (r, 0)),
        out_shape=jax.ShapeDtypeStruct((R, C), BF16),
        compiler_params=_params("parallel"),
        name="cast_bf16",
    )(w)


def _stack_low_rank(pairs):
    rank = max(w1.shape[1] for w1, _ in pairs)
    rank += (-rank) % LANES
    l1 = jnp.stack([jnp.pad(w1, ((0, 0), (0, rank - w1.shape[1]))) for w1, _ in pairs])
    l2 = jnp.stack([jnp.pad(w2, ((0, rank - w2.shape[0]), (0, 0))) for _, w2 in pairs])
    return l1.astype(BF16), l2.astype(BF16)


def kernel(x, g_pre_mix, g_post_mix, g_pre_ffn, g_post_ffn, ffn_w1, ffn_w2, da_wq, da_wk, da_wv, da_wo, da_lambda, da_subln, rw_mix, rw_wr, rw_wk, rw_wv, rw_wo, rw_w0, rw_w1, rw_w2, rw_a0, rw_a1, rw_a2, rw_g1, rw_g2, rw_kk, rw_ka, rw_rk, rw_lnx_g, rw_lnx_b):
    B, S, D = x.shape
    depth = g_pre_mix.shape[0]
    h = x.reshape(B * S, D)
    tabs = _rope_tables(S)
    for i in range(depth):
        j = i // N_MIXERS
        if i % N_MIXERS == 0:
            lambda_init = 0.8 - 0.6 * math.exp(-0.3 * i)
            wqkv = jnp.concatenate([_layer_bf16(w, j) for w in (da_wq, da_wk, da_wv)], axis=1)
            qkv = _qkv_proj(h, g_pre_mix[i], wqkv, tabs, S)
            att = _diff_attention(qkv, da_lambda[j], da_subln[j], lambda_init, B, S)
            h = _proj_norm_res(att, _layer_bf16(da_wo, j), g_post_mix[i], h)
        else:
            mix = rw_mix[j][jnp.array([0, 2, 3, 1, 4, 5])]
            g_in = g_pre_mix[i][None, :]
            wrkv = jnp.stack([_layer_bf16(w, j) for w in (rw_wr, rw_wk, rw_wv)])
            l1, l2 = _stack_low_rank([(rw_w1[j], rw_w2[j]), (rw_a1[j], rw_a2[j]), (rw_g1[j], rw_g2[j])])
            lb = jnp.stack([rw_w0[j], rw_a0[j], jnp.zeros_like(rw_a0[j])])[:, None, :]
            mixed, low = _rwkv_pre(h, g_in * (1.0 - mix), g_in * mix, l1, l2, lb, S)
            y = _wkv(_bmm(mixed, wrkv), low, rw_kk[j], rw_ka[j], rw_rk[j].reshape(-1), rw_lnx_g[j], rw_lnx_b[j], B, S)
            h = _proj_norm_res(y, _layer_bf16(rw_wo, j), g_post_mix[i], h)
        h = _ffn(h, g_pre_ffn[i], _layer_bf16(ffn_w1, i), _layer_bf16(ffn_w2, i), g_post_ffn[i])
    return h.reshape(B, S, D)
```

```python
import functools
import math

import jax
import jax.numpy as jnp
from jax import lax
from jax.experimental import pallas as pl
from jax.experimental.pallas import tpu as pltpu

F32 = jnp.float32
BF16 = jnp.bfloat16

CHUNK = 64
DA_HEADS = 8
DA_HEAD_DIM = 128
DA_V_DIM = 2 * DA_HEAD_DIM
ROT_DIM = DA_HEAD_DIM // 4
ROPE_THETA = 500000.0
DA_SUBLN_EPS = 1e-5
RW_HEAD = 64
GN_EPS = 64e-5
EPS = 1e-6
NEG_INF = -1e30
N_MIXERS = 2

VMEM_LIMIT_BYTES = 52 * 1024 * 1024
LANES = 128
WKV_CHUNK = 64


def _params(*sem):
    return pltpu.CompilerParams(dimension_semantics=sem, vmem_limit_bytes=VMEM_LIMIT_BYTES)


def _rms(x, eps):
    return x * lax.rsqrt(jnp.mean(x * x, axis=-1, keepdims=True) + eps)


def _dot(a, b):
    return jnp.dot(a, b, preferred_element_type=F32)


def _dot_nt(a, b):
    return lax.dot_general(a, b, (((1,), (1,)), ((), ())), preferred_element_type=F32)


def _dot_tn(a, b):
    return lax.dot_general(a, b, (((0,), (0,)), ((), ())), preferred_element_type=F32)


def _qkv_kernel(x_ref, g_ref, w_ref, tab_ref, o_ref, a_scr, *, n_sub):
    @pl.when(pl.program_id(1) == 0)
    def _():
        a_scr[...] = (_rms(x_ref[...], EPS) * g_ref[...]).astype(BF16)

    c, s1, s2 = tab_ref[0], tab_ref[1], tab_ref[2]
    half = ROT_DIM // 2
    for n in range(o_ref.shape[1] // n_sub):
        acc = _dot(a_scr[...], w_ref[:, n * n_sub:(n + 1) * n_sub])
        for h in range(n_sub // DA_HEAD_DIM):
            blk = acc[:, h * DA_HEAD_DIM:(h + 1) * DA_HEAD_DIM]
            up = pltpu.roll(blk, DA_HEAD_DIM - half, axis=1)
            dn = pltpu.roll(blk, half, axis=1)
            lanes = slice(n * n_sub + h * DA_HEAD_DIM, n * n_sub + (h + 1) * DA_HEAD_DIM)
            o_ref[:, lanes] = (blk * c + up * s1 + dn * s2).astype(o_ref.dtype)


def _qkv_proj(x, g, w, tabs, seq, *, tm=1024, tn=1024, n_sub=256):
    T, D = x.shape
    N = w.shape[1]
    n_q_tiles = (N // 3) // tn
    tiles_per_seq = seq // tm
    return pl.pallas_call(
        functools.partial(_qkv_kernel, n_sub=n_sub),
        grid=(T // tm, N // tn),
        in_specs=[
            pl.BlockSpec((tm, D), lambda i, j: (i, 0)),
            pl.BlockSpec((1, D), lambda i, j: (0, 0)),
            pl.BlockSpec((D, tn), lambda i, j: (0, j)),
            pl.BlockSpec((None, 3, tm, DA_HEAD_DIM),
                         lambda i, j: (j // n_q_tiles, 0, i % tiles_per_seq, 0)),
        ],
        out_specs=pl.BlockSpec((tm, tn), lambda i, j: (i, j)),
        out_shape=jax.ShapeDtypeStruct((T, N), BF16),
        scratch_shapes=[pltpu.VMEM((tm, D), BF16)],
        compiler_params=_params("parallel", "arbitrary"),
        name="qkv_proj",
    )(x, g.reshape(1, D), w, tabs)


def _rope_tables(seq):
    half = ROT_DIM // 2
    inv_freq = ROPE_THETA ** (-jnp.arange(half, dtype=F32) * (2.0 / ROT_DIM))
    ang = jnp.arange(seq, dtype=F32)[:, None] * inv_freq[None, :]
    cos, sin = jnp.cos(ang), jnp.sin(ang)
    rest = DA_HEAD_DIM - ROT_DIM
    c = jnp.concatenate([cos, cos, jnp.ones((seq, rest), F32)], axis=1)
    s1 = jnp.concatenate([-sin, jnp.zeros((seq, DA_HEAD_DIM - half), F32)], axis=1)
    s2 = jnp.concatenate([jnp.zeros((seq, half), F32), sin, jnp.zeros((seq, rest), F32)], axis=1)
    k_tab = jnp.stack([c, s1, s2])
    v_tab = jnp.stack([jnp.ones_like(c), jnp.zeros_like(c), jnp.zeros_like(c)])
    return jnp.stack([k_tab * (DA_HEAD_DIM ** -0.5), k_tab, v_tab])


def _attn_kernel(lam_ref, q_ref, k_ref, v_ref, g_ref, o_ref, *, blk, lambda_init):
    row = lax.broadcasted_iota(jnp.int32, (blk, blk), 0)
    col = lax.broadcasted_iota(jnp.int32, (blk, blk), 1)
    allowed = (col // CHUNK) <= (row // CHUNK)
    lv = lam_ref[...]
    lam = (jnp.exp(jnp.sum(lv[0:1] * lv[1:2], axis=-1, keepdims=True))
           - jnp.exp(jnp.sum(lv[2:3] * lv[3:4], axis=-1, keepdims=True)) + lambda_init)

    def scores(m, n):
        hs = slice(m * DA_HEAD_DIM, (m + 1) * DA_HEAD_DIM)
        past = n * blk
        q = q_ref[past:past + blk, hs]
        s_d = jnp.where(allowed, _dot_nt(q, k_ref[past:past + blk, hs]), NEG_INF)
        s_p = _dot_nt(q, k_ref[0:past, hs]) if n > 0 else None
        return s_d, s_p

    def softmax_pv(n, s_d, s_p):
        past = n * blk
        mx = jnp.max(s_d, axis=-1, keepdims=True)
        if n > 0:
            mx = jnp.maximum(mx, jnp.max(s_p, axis=-1, keepdims=True))
        p_d = jnp.exp(s_d - mx)
        l = jnp.sum(p_d, axis=-1, keepdims=True)
        acc = _dot(p_d.astype(BF16), v_ref[past:past + blk, :])
        if n > 0:
            p_p = jnp.exp(s_p - mx)
            l = l + jnp.sum(p_p, axis=-1, keepdims=True)
            acc = acc + _dot(p_p.astype(BF16), v_ref[0:past, :])
        return acc * (1.0 / l)

    n_blocks = k_ref.shape[0] // blk
    s_next = [scores(m, 0) for m in range(2)]
    for n in range(n_blocks):
        s = s_next
        if n + 1 < n_blocks:
            s_next = [scores(m, n + 1) for m in range(2)]
        o = softmax_pv(n, *s[0]) - lam * softmax_pv(n, *s[1])
        o = _rms(o, DA_SUBLN_EPS) * g_ref[...] * (1.0 - lambda_init)
        o_ref[n * blk:(n + 1) * blk, :] = o.astype(o_ref.dtype)


def _diff_attention(qkv, lam_vecs, subln_g, lambda_init, batch, seq, *, blk=256):
    T = qkv.shape[0]
    return pl.pallas_call(
        functools.partial(_attn_kernel, blk=blk, lambda_init=lambda_init),
        grid=(batch, DA_HEADS),
        in_specs=[
            pl.BlockSpec((4, DA_HEAD_DIM), lambda b, h: (0, 0)),
            pl.BlockSpec((seq, DA_V_DIM), lambda b, h: (b, h)),
            pl.BlockSpec((seq, DA_V_DIM), lambda b, h: (b, DA_HEADS + h)),
            pl.BlockSpec((seq, DA_V_DIM), lambda b, h: (b, 2 * DA_HEADS + h)),
            pl.BlockSpec((1, DA_V_DIM), lambda b, h: (0, 0)),
        ],
        out_specs=pl.BlockSpec((seq, DA_V_DIM), lambda b, h: (b, h)),
        out_shape=jax.ShapeDtypeStruct((T, DA_HEADS * DA_V_DIM), BF16),
        compiler_params=_params("parallel", "parallel"),
        name="diff_attn",
    )(lam_vecs, qkv, qkv, qkv, subln_g.reshape(1, DA_V_DIM))


def _proj_norm_res_kernel(a_ref, w_ref, g_ref, x_ref, o_ref):
    m = _dot(a_ref[...], w_ref[...])
    o_ref[...] = x_ref[...] + _rms(m, EPS) * g_ref[...]


def _proj_norm_res(a, w, g, resid, *, tm=512):
    T, K = a.shape
    N = w.shape[1]
    return pl.pallas_call(
        _proj_norm_res_kernel,
        grid=(T // tm,),
        in_specs=[
            pl.BlockSpec((tm, K), lambda i: (i, 0)),
            pl.BlockSpec((K, N), lambda i: (0, 0)),
            pl.BlockSpec((1, N), lambda i: (0, 0)),
            pl.BlockSpec((tm, N), lambda i: (i, 0)),
        ],
        out_specs=pl.BlockSpec((tm, N), lambda i: (i, 0)),
        out_shape=jax.ShapeDtypeStruct((T, N), F32),
        compiler_params=_params("parallel"),
        name="proj_norm_res",
    )(a, w, g.reshape(1, N), resid)


def _ffn_kernel(x_ref, g1_ref, w1_ref, w2_ref, g2_ref, o_ref, h_scr, acc_scr):
    f = pl.program_id(1)

    @pl.when(f == 0)
    def _():
        h_scr[...] = (_rms(x_ref[...], EPS) * g1_ref[...]).astype(BF16)
        acc_scr[...] = jnp.zeros_like(acc_scr)

    u = jnp.maximum(_dot(h_scr[...], w1_ref[...]), 0.0)
    acc_scr[...] += _dot((u * u).astype(BF16), w2_ref[...])

    @pl.when(f == pl.num_programs(1) - 1)
    def _():
        o_ref[...] = x_ref[...] + _rms(acc_scr[...], EPS) * g2_ref[...]


def _ffn(x, g1, w1, w2, g2, *, tm=512, tf=1024):
    T, D = x.shape
    F = w1.shape[1]
    return pl.pallas_call(
        _ffn_kernel,
        grid=(T // tm, F // tf),
        in_specs=[
            pl.BlockSpec((tm, D), lambda i, f: (i, 0)),
            pl.BlockSpec((1, D), lambda i, f: (0, 0)),
            pl.BlockSpec((D, tf), lambda i, f: (0, f)),
            pl.BlockSpec((tf, D), lambda i, f: (f, 0)),
            pl.BlockSpec((1, D), lambda i, f: (0, 0)),
        ],
        out_specs=pl.BlockSpec((tm, D), lambda i, f: (i, 0)),
        out_shape=jax.ShapeDtypeStruct((T, D), F32),
        scratch_shapes=[pltpu.VMEM((tm, D), BF16), pltpu.VMEM((tm, D), F32)],
        compiler_params=_params("parallel", "arbitrary"),
        name="ffn",
    )(x, g1.reshape(1, D), w1, w2, g2.reshape(1, D))


def _rwkv_pre_kernel(x_ref, xp_ref, u_ref, w_ref, l1_ref, l2_ref, lb_ref, mixed_ref, low_ref, *, tiles_per_seq):
    i = pl.program_id(0)
    n_dense = mixed_ref.shape[0]
    xs = _rms(x_ref[...], EPS)
    sub = xp_ref.shape[0]
    xp = _rms(xp_ref[sub - 1:sub, :], EPS)
    xp = jnp.where(i % tiles_per_seq == 0, jnp.zeros_like(xp), xp)
    row = lax.broadcasted_iota(jnp.int32, (xs.shape[0], 1), 0)
    prev = jnp.where(row == 0, xp, pltpu.roll(xs, 1, axis=0))

    def mixed(c):
        return (xs * u_ref[c:c + 1, :] + prev * w_ref[c:c + 1, :]).astype(BF16)

    def sigmoid(z):
        return 0.5 * jnp.tanh(0.5 * z) + 0.5

    for c in range(n_dense):
        mixed_ref[c] = mixed(c)
    pre = (jnp.tanh, lambda t: t, sigmoid)
    post = (lambda z: -math.exp(-0.5) * sigmoid(z + lb_ref[0]), lambda z: sigmoid(z + lb_ref[1]), lambda z: z)
    t = [_dot(mixed(n_dense + l), l1_ref[l]) for l in range(3)]
    t = [pre[l](t[l]).astype(BF16) for l in range(3)]
    z = [_dot(t[l], l2_ref[l]) for l in range(3)]
    for l in range(3):
        low_ref[l] = post[l](z[l])


def _rwkv_pre(x, u, w, l1, l2, lb, seq, *, tm=256, sub=8):
    T, D = x.shape
    n_proj = u.shape[0]
    n_low, _, R = l1.shape
    n_dense = n_proj - n_low
    per = tm // sub
    whole = lambda shape: pl.BlockSpec(shape, lambda i: (0,) * len(shape))
    return pl.pallas_call(
        functools.partial(_rwkv_pre_kernel, tiles_per_seq=seq // tm),
        grid=(T // tm,),
        in_specs=[
            pl.BlockSpec((tm, D), lambda i: (i, 0)),
            pl.BlockSpec((sub, D), lambda i: (jnp.maximum(i * per - 1, 0), 0)),
            whole((n_proj, D)), whole((n_proj, D)),
            whole((n_low, D, R)), whole((n_low, R, D)), whole((n_low, 1, D)),
        ],
        out_specs=[pl.BlockSpec((n_dense, tm, D), lambda i: (0, i, 0)),
                   pl.BlockSpec((n_low, tm, D), lambda i: (0, i, 0))],
        out_shape=[jax.ShapeDtypeStruct((n_dense, T, D), BF16), jax.ShapeDtypeStruct((n_low, T, D), F32)],
        compiler_params=_params("parallel"),
        name="rwkv_pre",
    )(x, x, u, w, l1, l2, lb)


def _bmm_kernel(a_ref, w_ref, o_ref):
    o_ref[...] = _dot(a_ref[...], w_ref[...])


def _bmm(a, w, *, tm=1024, tn=1024):
    P, K, N = w.shape
    T = a.shape[1]
    return pl.pallas_call(
        _bmm_kernel,
        grid=(P, T // tm, N // tn),
        in_specs=[
            pl.BlockSpec((None, tm, K), lambda p, i, j: (p, i, 0)),
            pl.BlockSpec((None, K, tn), lambda p, i, j: (p, 0, j)),
        ],
        out_specs=pl.BlockSpec((None, tm, tn), lambda p, i, j: (p, i, j)),
        out_shape=jax.ShapeDtypeStruct((P, T, N), F32),
        compiler_params=_params("parallel", "parallel", "arbitrary"),
        name="rkv_proj",
    )(a, w)


def _cumsum_rows(tri, x):
    hi = x.astype(BF16)
    r1 = x - hi.astype(F32)
    mid = r1.astype(BF16)
    lo = (r1 - mid.astype(F32)).astype(BF16)
    return _dot(tri, hi) + _dot(tri, mid) + _dot(tri, lo)


def _wkv_kernel(r_ref, k_ref, v_ref, lw_ref, a_ref, g_ref, kkp_ref, kap_ref, rkp_ref, lng_ref, lnb_ref,
                o_ref, s_scr, *, n_chunks):
    C = WKV_CHUNK
    N = RW_HEAD
    PW = 2 * N
    pairs = s_scr.shape[0]

    @pl.when(pl.program_id(2) == 0)
    def _():
        s_scr[...] = jnp.zeros_like(s_scr)

    row = lax.broadcasted_iota(jnp.int32, (C, PW), 0)
    lane = lax.broadcasted_iota(jnp.int32, (C, PW), 1)
    col = lane % N
    lo = lane < N
    strict = row > col
    incl = row >= col
    incl2 = jnp.concatenate([incl, incl], axis=1)
    eye = (row == col).astype(F32)
    tri = (lax.broadcasted_iota(jnp.int32, (C, C), 0) >= lax.broadcasted_iota(jnp.int32, (C, C), 1)).astype(BF16)
    base_f = ((row // 2) == (col // 2)).astype(F32)
    level_f = []
    s = 2
    while s < C:
        level_f.append((((row // (2 * s)) == (col // (2 * s))) & ((row // s) != (col // s))).astype(F32))
        s *= 2
    diag_blocks = ((lax.broadcasted_iota(jnp.int32, (PW, PW), 0) < N)
                   == (lax.broadcasted_iota(jnp.int32, (PW, PW), 1) < N))

    def bd(z):
        zero = jnp.zeros_like(z)
        return jnp.concatenate([jnp.where(lo, z, zero), jnp.where(lo, zero, z)], axis=0)

    def seg_sum(x):
        s_lo = jnp.sum(jnp.where(lo, x, 0.0), axis=-1, keepdims=True)
        s_hi = jnp.sum(jnp.where(lo, 0.0, x), axis=-1, keepdims=True)
        return jnp.where(lo, s_lo, s_hi)

    def chunk(c, _):
        t0 = pl.multiple_of(c * C, C)
        sl = pl.ds(t0, C)
        lw_all = lw_ref[sl, :]
        L_all = _cumsum_rows(tri, lw_all)
        psl = [slice(p * PW, (p + 1) * PW) for p in range(pairs)]
        P = range(pairs)
        r = [r_ref[sl, ps] for ps in psl]
        k = [k_ref[sl, ps] for ps in psl]
        a = [a_ref[sl, ps] for ps in psl]
        vb = [v_ref[sl, ps].astype(BF16) for ps in psl]
        L = [L_all[:, ps] for ps in psl]
        Lc = [x[C - 1:C, :] for x in L]
        km = [k[p] * (1.0 + (a[p] - 1.0) * kap_ref[:, psl[p]]) for p in P]
        kkr = [k[p] * kkp_ref[:, psl[p]] for p in P]
        kk = [kkr[p] * lax.rsqrt(jnp.maximum(seg_sum(kkr[p] * kkr[p]), 1e-24)) for p in P]
        b = [kk[p] * a[p] for p in P]
        e_nl = [jnp.exp(-L[p]) for p in P]
        e_end = [jnp.exp(Lc[p] - L[p]) for p in P]
        lhs = [jnp.concatenate([-kk[p] * jnp.exp(L[p] - lw_all[:, psl[p]]), r[p] * jnp.exp(L[p])],
                               axis=0).astype(BF16) for p in P]
        rhs = [jnp.concatenate([bd((b[p] * e_nl[p]).astype(BF16)), bd((km[p] * e_nl[p]).astype(BF16))],
                               axis=0) for p in P]
        tn_rhs = [jnp.concatenate([(b[p] * e_end[p]).astype(BF16), (km[p] * e_end[p]).astype(BF16)], axis=0)
                  for p in P]
        G = [_dot_nt(lhs[p], rhs[p]) for p in P]
        S0 = [s_scr[p] for p in P]
        SA = [_dot_nt(lhs[p], S0[p].astype(BF16)) for p in P]
        A_ab = [jnp.where(strict, G[p][:C, :PW], 0.0) for p in P]
        A_ak = [jnp.where(strict, G[p][:C, PW:], 0.0).astype(BF16) for p in P]
        P_rbk = [jnp.where(incl2, G[p][C:, :], 0.0).astype(BF16) for p in P]
        vbd = [bd(vb[p]) for p in P]
        W = [SA[p][:C] + _dot(A_ak[p], vbd[p]) for p in P]
        X = [eye + A_ab[p] * base_f for p in P]
        for lvl in level_f:
            Xb = [X[p].astype(BF16) for p in P]
            inner = [_dot((A_ab[p] * lvl).astype(BF16), bd(Xb[p])).astype(BF16) for p in P]
            X = [X[p] + _dot(Xb[p], bd(inner[p])) for p in P]
        ub = [_dot(X[p].astype(BF16), bd(W[p].astype(BF16))).astype(BF16) for p in P]
        Y = [SA[p][C:] + _dot(P_rbk[p], jnp.concatenate([bd(ub[p]), vbd[p]], axis=0)) for p in P]
        for p in P:
            upd = _dot_tn(jnp.concatenate([ub[p], vb[p]], axis=0), tn_rhs[p])
            s_scr[p] = jnp.where(diag_blocks, S0[p] * jnp.exp(Lc[p]) + upd, 0.0)
        for p in P:
            ps = psl[p]
            yc = Y[p] - seg_sum(Y[p]) * (1.0 / N)
            var = seg_sum(yc * yc) * (1.0 / N)
            yn = yc * lax.rsqrt(var + GN_EPS) * lng_ref[:, ps] + lnb_ref[:, ps]
            bonus = seg_sum(r[p] * km[p] * rkp_ref[:, ps]) * v_ref[sl, ps]
            o_ref[sl, ps] = ((yn + bonus) * g_ref[sl, ps]).astype(o_ref.dtype)
        return 0

    lax.fori_loop(0, n_chunks, chunk, 0)


def _wkv(rkv, low, kkp, kap, rkp, lng, lnb, batch, seq, *, hw=2048, tb=256):
    _, T, D = rkv.shape
    nt = seq // tb
    slot = lambda c: pl.BlockSpec((None, tb, hw), lambda b, j, t: (c, b * nt + t, j))
    par = pl.BlockSpec((1, hw), lambda b, j, t: (0, j))
    return pl.pallas_call(
        functools.partial(_wkv_kernel, n_chunks=tb // WKV_CHUNK),
        grid=(batch, D // hw, nt),
        in_specs=[slot(c) for c in range(3)] * 2 + [par] * 5,
        out_specs=pl.BlockSpec((tb, hw), lambda b, j, t: (b * nt + t, j)),
        out_shape=jax.ShapeDtypeStruct((T, D), BF16),
        scratch_shapes=[pltpu.VMEM((hw // (2 * RW_HEAD), 2 * RW_HEAD, 2 * RW_HEAD), F32)],
        compiler_params=_params("parallel", "parallel", "arbitrary"),
        name="wkv7",
    )(rkv, rkv, rkv, low, low, low, kkp.reshape(1, D), kap.reshape(1, D), rkp.reshape(1, D),
      lng.reshape(1, D), lnb.reshape(1, D))


def _cast_kernel(w_ref, o_ref):
    o_ref[...] = w_ref[...].astype(o_ref.dtype)


CAST_BLOCK_BYTES = 8 * 1024 * 1024


def _layer_bf16(w, layer):
    _, R, C = w.shape
    br = min(R, max(16, CAST_BLOCK_BYTES // (4 * C)))
    return pl.pallas_call(
        _cast_kernel,
        grid=(R // br,),
        in_specs=[pl.BlockSpec((None, br, C), lambda r: (layer, r, 0))],
        out_specs=pl.BlockSpec((br, C), lambda r: (r, 0)),
        out_shape=jax.ShapeDtypeStruct((R, C), BF16),
        compiler_params=_params("parallel"),
        name="cast_bf16",
    )(w)


def _stack_low_rank(pairs):
    rank = max(w1.shape[1] for w1, _ in pairs)
    rank += (-rank) % LANES
    l1 = jnp.stack([jnp.pad(w1, ((0, 0), (0, rank - w1.shape[1]))) for w1, _ in pairs])
    l2 = jnp.stack([jnp.pad(w2, ((0, rank - w2.shape[0]), (0, 0))) for _, w2 in pairs])
    return l1.astype(BF16), l2.astype(BF16)


def kernel(x, g_pre_mix, g_post_mix, g_pre_ffn, g_post_ffn, ffn_w1, ffn_w2, da_wq, da_wk, da_wv, da_wo, da_lambda, da_subln, rw_mix, rw_wr, rw_wk, rw_wv, rw_wo, rw_w0, rw_w1, rw_w2, rw_a0, rw_a1, rw_a2, rw_g1, rw_g2, rw_kk, rw_ka, rw_rk, rw_lnx_g, rw_lnx_b):
    B, S, D = x.shape
    depth = g_pre_mix.shape[0]
    h = x.reshape(B * S, D)
    tabs = _rope_tables(S)
    for i in range(depth):
        j = i // N_MIXERS
        if i % N_MIXERS == 0:
            lambda_init = 0.8 - 0.6 * math.exp(-0.3 * i)
            wqkv = jnp.concatenate([_layer_bf16(w, j) for w in (da_wq, da_wk, da_wv)], axis=1)
            qkv = _qkv_proj(h, g_pre_mix[i], wqkv, tabs, S)
            att = _diff_attention(qkv, da_lambda[j], da_subln[j], lambda_init, B, S)
            h = _proj_norm_res(att, _layer_bf16(da_wo, j), g_post_mix[i], h)
        else:
            mix = rw_mix[j][jnp.array([0, 2, 3, 1, 4, 5])]
            g_in = g_pre_mix[i][None, :]
            wrkv = jnp.stack([_layer_bf16(w, j) for w in (rw_wr, rw_wk, rw_wv)])
            l1, l2 = _stack_low_rank([(rw_w1[j], rw_w2[j]), (rw_a1[j], rw_a2[j]), (rw_g1[j], rw_g2[j])])
            lb = jnp.stack([rw_w0[j], rw_a0[j], jnp.zeros_like(rw_a0[j])])[:, None, :]
            mixed, low = _rwkv_pre(h, g_in * (1.0 - mix), g_in * mix, l1, l2, lb, S)
            y = _wkv(_bmm(mixed, wrkv), low, rw_kk[j], rw_ka[j], rw_rk[j].reshape(-1), rw_lnx_g[j], rw_lnx_b[j], B, S)
            h = _proj_norm_res(y, _layer_bf16(rw_wo, j), g_post_mix[i], h)
        h = _ffn(h, g_pre_ffn[i], _layer_bf16(ffn_w1, i), _layer_bf16(ffn_w2, i), g_post_ffn[i])
    return h.reshape(B, S, D)
```

```python
import functools
import math

import jax
import jax.numpy as jnp
from jax import lax
from jax.experimental import pallas as pl
from jax.experimental.pallas import tpu as pltpu

F32 = jnp.float32
BF16 = jnp.bfloat16

CHUNK = 64
DA_HEADS = 8
DA_HEAD_DIM = 128
DA_V_DIM = 2 * DA_HEAD_DIM
ROT_DIM = DA_HEAD_DIM // 4
ROPE_THETA = 500000.0
DA_SUBLN_EPS = 1e-5
RW_HEAD = 64
GN_EPS = 64e-5
EPS = 1e-6
NEG_INF = -1e30
N_MIXERS = 2

VMEM_LIMIT_BYTES = 52 * 1024 * 1024
LANES = 128
WKV_CHUNK = 64


def _params(*sem):
    return pltpu.CompilerParams(dimension_semantics=sem, vmem_limit_bytes=VMEM_LIMIT_BYTES)


def _rms(x, eps):
    return x * lax.rsqrt(jnp.mean(x * x, axis=-1, keepdims=True) + eps)


def _dot(a, b):
    return jnp.dot(a, b, preferred_element_type=F32)


def _dot_nt(a, b):
    return lax.dot_general(a, b, (((1,), (1,)), ((), ())), preferred_element_type=F32)


def _dot_tn(a, b):
    return lax.dot_general(a, b, (((0,), (0,)), ((), ())), preferred_element_type=F32)


def _qkv_kernel(x_ref, g_ref, w_ref, tab_ref, o_ref, a_scr, *, n_sub):
    @pl.when(pl.program_id(1) == 0)
    def _():
        a_scr[...] = (_rms(x_ref[...], EPS) * g_ref[...]).astype(BF16)

    c, s1, s2 = tab_ref[0], tab_ref[1], tab_ref[2]
    half = ROT_DIM // 2
    for n in range(o_ref.shape[1] // n_sub):
        acc = _dot(a_scr[...], w_ref[:, n * n_sub:(n + 1) * n_sub])
        for h in range(n_sub // DA_HEAD_DIM):
            blk = acc[:, h * DA_HEAD_DIM:(h + 1) * DA_HEAD_DIM]
            up = pltpu.roll(blk, DA_HEAD_DIM - half, axis=1)
            dn = pltpu.roll(blk, half, axis=1)
            lanes = slice(n * n_sub + h * DA_HEAD_DIM, n * n_sub + (h + 1) * DA_HEAD_DIM)
            o_ref[:, lanes] = (blk * c + up * s1 + dn * s2).astype(o_ref.dtype)


def _qkv_proj(x, g, w, tabs, seq, *, tm=1024, tn=2048, n_sub=256):
    T, D = x.shape
    N = w.shape[1]
    n_q_tiles = (N // 3) // tn
    tiles_per_seq = seq // tm
    return pl.pallas_call(
        functools.partial(_qkv_kernel, n_sub=n_sub),
        grid=(T // tm, N // tn),
        in_specs=[
            pl.BlockSpec((tm, D), lambda i, j: (i, 0)),
            pl.BlockSpec((1, D), lambda i, j: (0, 0)),
            pl.BlockSpec((D, tn), lambda i, j: (0, j)),
            pl.BlockSpec((None, 3, tm, DA_HEAD_DIM),
                         lambda i, j: (j // n_q_tiles, 0, i % tiles_per_seq, 0)),
        ],
        out_specs=pl.BlockSpec((tm, tn), lambda i, j: (i, j)),
        out_shape=jax.ShapeDtypeStruct((T, N), BF16),
        scratch_shapes=[pltpu.VMEM((tm, D), BF16)],
        compiler_params=_params("parallel", "arbitrary"),
        name="qkv_proj",
    )(x, g.reshape(1, D), w, tabs)


def _rope_tables(seq):
    half = ROT_DIM // 2
    inv_freq = ROPE_THETA ** (-jnp.arange(half, dtype=F32) * (2.0 / ROT_DIM))
    ang = jnp.arange(seq, dtype=F32)[:, None] * inv_freq[None, :]
    cos, sin = jnp.cos(ang), jnp.sin(ang)
    rest = DA_HEAD_DIM - ROT_DIM
    c = jnp.concatenate([cos, cos, jnp.ones((seq, rest), F32)], axis=1)
    s1 = jnp.concatenate([-sin, jnp.zeros((seq, DA_HEAD_DIM - half), F32)], axis=1)
    s2 = jnp.concatenate([jnp.zeros((seq, half), F32), sin, jnp.zeros((seq, rest), F32)], axis=1)
    k_tab = jnp.stack([c, s1, s2])
    v_tab = jnp.stack([jnp.ones_like(c), jnp.zeros_like(c), jnp.zeros_like(c)])
    return jnp.stack([k_tab * (DA_HEAD_DIM ** -0.5), k_tab, v_tab])


def _attn_kernel(lam_ref, q_ref, k_ref, v_ref, g_ref, o_ref, *, blk, lambda_init):
    row = lax.broadcasted_iota(jnp.int32, (blk, blk), 0)
    col = lax.broadcasted_iota(jnp.int32, (blk, blk), 1)
    allowed = (col // CHUNK) <= (row // CHUNK)
    lv = lam_ref[...]
    lam = (jnp.exp(jnp.sum(lv[0:1] * lv[1:2], axis=-1, keepdims=True))
           - jnp.exp(jnp.sum(lv[2:3] * lv[3:4], axis=-1, keepdims=True)) + lambda_init)

    def scores(m, n):
        hs = slice(m * DA_HEAD_DIM, (m + 1) * DA_HEAD_DIM)
        past = n * blk
        q = q_ref[past:past + blk, hs]
        s_d = jnp.where(allowed, _dot_nt(q, k_ref[past:past + blk, hs]), NEG_INF)
        s_p = _dot_nt(q, k_ref[0:past, hs]) if n > 0 else None
        return s_d, s_p

    def softmax_pv(n, s_d, s_p):
        past = n * blk
        mx = jnp.max(s_d, axis=-1, keepdims=True)
        if n > 0:
            mx = jnp.maximum(mx, jnp.max(s_p, axis=-1, keepdims=True))
        p_d = jnp.exp(s_d - mx)
        l = jnp.sum(p_d, axis=-1, keepdims=True)
        acc = _dot(p_d.astype(BF16), v_ref[past:past + blk, :])
        if n > 0:
            p_p = jnp.exp(s_p - mx)
            l = l + jnp.sum(p_p, axis=-1, keepdims=True)
            acc = acc + _dot(p_p.astype(BF16), v_ref[0:past, :])
        return acc * (1.0 / l)

    n_blocks = k_ref.shape[0] // blk
    s_next = [scores(m, 0) for m in range(2)]
    for n in range(n_blocks):
        s = s_next
        if n + 1 < n_blocks:
            s_next = [scores(m, n + 1) for m in range(2)]
        o = softmax_pv(n, *s[0]) - lam * softmax_pv(n, *s[1])
        o = _rms(o, DA_SUBLN_EPS) * g_ref[...] * (1.0 - lambda_init)
        o_ref[n * blk:(n + 1) * blk, :] = o.astype(o_ref.dtype)


def _diff_attention(qkv, lam_vecs, subln_g, lambda_init, batch, seq, *, blk=256):
    T = qkv.shape[0]
    return pl.pallas_call(
        functools.partial(_attn_kernel, blk=blk, lambda_init=lambda_init),
        grid=(batch, DA_HEADS),
        in_specs=[
            pl.BlockSpec((4, DA_HEAD_DIM), lambda b, h: (0, 0)),
            pl.BlockSpec((seq, DA_V_DIM), lambda b, h: (b, h)),
            pl.BlockSpec((seq, DA_V_DIM), lambda b, h: (b, DA_HEADS + h)),
            pl.BlockSpec((seq, DA_V_DIM), lambda b, h: (b, 2 * DA_HEADS + h)),
            pl.BlockSpec((1, DA_V_DIM), lambda b, h: (0, 0)),
        ],
        out_specs=pl.BlockSpec((seq, DA_V_DIM), lambda b, h: (b, h)),
        out_shape=jax.ShapeDtypeStruct((T, DA_HEADS * DA_V_DIM), BF16),
        compiler_params=_params("parallel", "parallel"),
        name="diff_attn",
    )(lam_vecs, qkv, qkv, qkv, subln_g.reshape(1, DA_V_DIM))


def _proj_norm_res_kernel(a_ref, w_ref, g_ref, x_ref, o_ref):
    m = _dot(a_ref[...], w_ref[...])
    o_ref[...] = x_ref[...] + _rms(m, EPS) * g_ref[...]


def _proj_norm_res(a, w, g, resid, *, tm=512):
    T, K = a.shape
    N = w.shape[1]
    return pl.pallas_call(
        _proj_norm_res_kernel,
        grid=(T // tm,),
        in_specs=[
            pl.BlockSpec((tm, K), lambda i: (i, 0)),
            pl.BlockSpec((K, N), lambda i: (0, 0)),
            pl.BlockSpec((1, N), lambda i: (0, 0)),
            pl.BlockSpec((tm, N), lambda i: (i, 0)),
        ],
        out_specs=pl.BlockSpec((tm, N), lambda i: (i, 0)),
        out_shape=jax.ShapeDtypeStruct((T, N), F32),
        compiler_params=_params("parallel"),
        name="proj_norm_res",
    )(a, w, g.reshape(1, N), resid)


def _ffn_kernel(x_ref, g1_ref, w1_ref, w2_ref, g2_ref, o_ref, h_scr, acc_scr):
    f = pl.program_id(1)

    @pl.when(f == 0)
    def _():
        h_scr[...] = (_rms(x_ref[...], EPS) * g1_ref[...]).astype(BF16)
        acc_scr[...] = jnp.zeros_like(acc_scr)

    u = jnp.maximum(_dot(h_scr[...], w1_ref[...]), 0.0)
    acc_scr[...] += _dot((u * u).astype(BF16), w2_ref[...])

    @pl.when(f == pl.num_programs(1) - 1)
    def _():
        o_ref[...] = x_ref[...] + _rms(acc_scr[...], EPS) * g2_ref[...]


def _ffn(x, g1, w1, w2, g2, *, tm=512, tf=1024):
    T, D = x.shape
    F = w1.shape[1]
    return pl.pallas_call(
        _ffn_kernel,
        grid=(T // tm, F // tf),
        in_specs=[
            pl.BlockSpec((tm, D), lambda i, f: (i, 0)),
            pl.BlockSpec((1, D), lambda i, f: (0, 0)),
            pl.BlockSpec((D, tf), lambda i, f: (0, f)),
            pl.BlockSpec((tf, D), lambda i, f: (f, 0)),
            pl.BlockSpec((1, D), lambda i, f: (0, 0)),
        ],
        out_specs=pl.BlockSpec((tm, D), lambda i, f: (i, 0)),
        out_shape=jax.ShapeDtypeStruct((T, D), F32),
        scratch_shapes=[pltpu.VMEM((tm, D), BF16), pltpu.VMEM((tm, D), F32)],
        compiler_params=_params("parallel", "arbitrary"),
        name="ffn",
    )(x, g1.reshape(1, D), w1, w2, g2.reshape(1, D))


def _rwkv_pre_kernel(x_ref, xp_ref, u_ref, w_ref, l1_ref, l2_ref, lb_ref, mixed_ref, low_ref, *, tiles_per_seq):
    i = pl.program_id(0)
    n_dense = mixed_ref.shape[0]
    xs = _rms(x_ref[...], EPS)
    sub = xp_ref.shape[0]
    xp = _rms(xp_ref[sub - 1:sub, :], EPS)
    xp = jnp.where(i % tiles_per_seq == 0, jnp.zeros_like(xp), xp)
    row = lax.broadcasted_iota(jnp.int32, (xs.shape[0], 1), 0)
    prev = jnp.where(row == 0, xp, pltpu.roll(xs, 1, axis=0))

    def mixed(c):
        return (xs * u_ref[c:c + 1, :] + prev * w_ref[c:c + 1, :]).astype(BF16)

    def sigmoid(z):
        return 0.5 * jnp.tanh(0.5 * z) + 0.5

    for c in range(n_dense):
        mixed_ref[c] = mixed(c)
    pre = (jnp.tanh, lambda t: t, sigmoid)
    post = (lambda z: -math.exp(-0.5) * sigmoid(z + lb_ref[0]), lambda z: sigmoid(z + lb_ref[1]), lambda z: z)
    t = [_dot(mixed(n_dense + l), l1_ref[l]) for l in range(3)]
    t = [pre[l](t[l]).astype(BF16) for l in range(3)]
    z = [_dot(t[l], l2_ref[l]) for l in range(3)]
    for l in range(3):
        low_ref[l] = post[l](z[l])


def _rwkv_pre(x, u, w, l1, l2, lb, seq, *, tm=256, sub=8):
    T, D = x.shape
    n_proj = u.shape[0]
    n_low, _, R = l1.shape
    n_dense = n_proj - n_low
    per = tm // sub
    whole = lambda shape: pl.BlockSpec(shape, lambda i: (0,) * len(shape))
    return pl.pallas_call(
        functools.partial(_rwkv_pre_kernel, tiles_per_seq=seq // tm),
        grid=(T // tm,),
        in_specs=[
            pl.BlockSpec((tm, D), lambda i: (i, 0)),
            pl.BlockSpec((sub, D), lambda i: (jnp.maximum(i * per - 1, 0), 0)),
            whole((n_proj, D)), whole((n_proj, D)),
            whole((n_low, D, R)), whole((n_low, R, D)), whole((n_low, 1, D)),
        ],
        out_specs=[pl.BlockSpec((n_dense, tm, D), lambda i: (0, i, 0)),
                   pl.BlockSpec((n_low, tm, D), lambda i: (0, i, 0))],
        out_shape=[jax.ShapeDtypeStruct((n_dense, T, D), BF16), jax.ShapeDtypeStruct((n_low, T, D), F32)],
        compiler_params=_params("parallel"),
        name="rwkv_pre",
    )(x, x, u, w, l1, l2, lb)


def _bmm_kernel(a_ref, w_ref, o_ref):
    o_ref[...] = _dot(a_ref[...], w_ref[...])


def _bmm(a, w, *, tm=1024, tn=2048):
    P, K, N = w.shape
    T = a.shape[1]
    return pl.pallas_call(
        _bmm_kernel,
        grid=(P, T // tm, N // tn),
        in_specs=[
            pl.BlockSpec((None, tm, K), lambda p, i, j: (p, i, 0)),
            pl.BlockSpec((None, K, tn), lambda p, i, j: (p, 0, j)),
        ],
        out_specs=pl.BlockSpec((None, tm, tn), lambda p, i, j: (p, i, j)),
        out_shape=jax.ShapeDtypeStruct((P, T, N), F32),
        compiler_params=_params("parallel", "parallel", "arbitrary"),
        name="rkv_proj",
    )(a, w)


def _cumsum_rows(tri, x):
    hi = x.astype(BF16)
    r1 = x - hi.astype(F32)
    mid = r1.astype(BF16)
    lo = (r1 - mid.astype(F32)).astype(BF16)
    return _dot(tri, hi) + _dot(tri, mid) + _dot(tri, lo)


def _wkv_kernel(r_ref, k_ref, v_ref, lw_ref, a_ref, g_ref, kkp_ref, kap_ref, rkp_ref, lng_ref, lnb_ref,
                o_ref, s_scr, *, n_chunks):
    C = WKV_CHUNK
    N = RW_HEAD
    PW = 2 * N
    pairs = s_scr.shape[0]

    @pl.when(pl.program_id(2) == 0)
    def _():
        s_scr[...] = jnp.zeros_like(s_scr)

    row = lax.broadcasted_iota(jnp.int32, (C, PW), 0)
    lane = lax.broadcasted_iota(jnp.int32, (C, PW), 1)
    col = lane % N
    lo = lane < N
    strict = row > col
    incl = row >= col
    incl2 = jnp.concatenate([incl, incl], axis=1)
    eye = (row == col).astype(F32)
    tri = (lax.broadcasted_iota(jnp.int32, (C, C), 0) >= lax.broadcasted_iota(jnp.int32, (C, C), 1)).astype(BF16)
    base_f = ((row // 2) == (col // 2)).astype(F32)
    level_f = []
    s = 2
    while s < C:
        level_f.append((((row // (2 * s)) == (col // (2 * s))) & ((row // s) != (col // s))).astype(F32))
        s *= 2
    diag_blocks = ((lax.broadcasted_iota(jnp.int32, (PW, PW), 0) < N)
                   == (lax.broadcasted_iota(jnp.int32, (PW, PW), 1) < N))

    def bd(z):
        zero = jnp.zeros_like(z)
        return jnp.concatenate([jnp.where(lo, z, zero), jnp.where(lo, zero, z)], axis=0)

    def seg_sum(x):
        s_lo = jnp.sum(jnp.where(lo, x, 0.0), axis=-1, keepdims=True)
        s_hi = jnp.sum(jnp.where(lo, 0.0, x), axis=-1, keepdims=True)
        return jnp.where(lo, s_lo, s_hi)

    def chunk(c, _):
        t0 = pl.multiple_of(c * C, C)
        sl = pl.ds(t0, C)
        lw_all = lw_ref[sl, :]
        L_all = _cumsum_rows(tri, lw_all)
        psl = [slice(p * PW, (p + 1) * PW) for p in range(pairs)]
        P = range(pairs)
        r = [r_ref[sl, ps] for ps in psl]
        k = [k_ref[sl, ps] for ps in psl]
        a = [a_ref[sl, ps] for ps in psl]
        vb = [v_ref[sl, ps].astype(BF16) for ps in psl]
        L = [L_all[:, ps] for ps in psl]
        Lc = [x[C - 1:C, :] for x in L]
        km = [k[p] * (1.0 + (a[p] - 1.0) * kap_ref[:, psl[p]]) for p in P]
        kkr = [k[p] * kkp_ref[:, psl[p]] for p in P]
        kk = [kkr[p] * lax.rsqrt(jnp.maximum(seg_sum(kkr[p] * kkr[p]), 1e-24)) for p in P]
        b = [kk[p] * a[p] for p in P]
        e_nl = [jnp.exp(-L[p]) for p in P]
        e_end = [jnp.exp(Lc[p] - L[p]) for p in P]
        lhs = [jnp.concatenate([-kk[p] * jnp.exp(L[p] - lw_all[:, psl[p]]), r[p] * jnp.exp(L[p])],
                               axis=0).astype(BF16) for p in P]
        rhs = [jnp.concatenate([bd((b[p] * e_nl[p]).astype(BF16)), bd((km[p] * e_nl[p]).astype(BF16))],
                               axis=0) for p in P]
        tn_rhs = [jnp.concatenate([(b[p] * e_end[p]).astype(BF16), (km[p] * e_end[p]).astype(BF16)], axis=0)
                  for p in P]
        G = [_dot_nt(lhs[p], rhs[p]) for p in P]
        S0 = [s_scr[p] for p in P]
        SA = [_dot_nt(lhs[p], S0[p].astype(BF16)) for p in P]
        A_ab = [jnp.where(strict, G[p][:C, :PW], 0.0) for p in P]
        A_ak = [jnp.where(strict, G[p][:C, PW:], 0.0).astype(BF16) for p in P]
        P_rbk = [jnp.where(incl2, G[p][C:, :], 0.0).astype(BF16) for p in P]
        vbd = [bd(vb[p]) for p in P]
        W = [SA[p][:C] + _dot(A_ak[p], vbd[p]) for p in P]
        X = [eye + A_ab[p] * base_f for p in P]
        for lvl in level_f:
            Xb = [X[p].astype(BF16) for p in P]
            inner = [_dot((A_ab[p] * lvl).astype(BF16), bd(Xb[p])).astype(BF16) for p in P]
            X = [X[p] + _dot(Xb[p], bd(inner[p])) for p in P]
        ub = [_dot(X[p].astype(BF16), bd(W[p].astype(BF16))).astype(BF16) for p in P]
        Y = [SA[p][C:] + _dot(P_rbk[p], jnp.concatenate([bd(ub[p]), vbd[p]], axis=0)) for p in P]
        for p in P:
            upd = _dot_tn(jnp.concatenate([ub[p], vb[p]], axis=0), tn_rhs[p])
            s_scr[p] = jnp.where(diag_blocks, S0[p] * jnp.exp(Lc[p]) + upd, 0.0)
        for p in P:
            ps = psl[p]
            yc = Y[p] - seg_sum(Y[p]) * (1.0 / N)
            var = seg_sum(yc * yc) * (1.0 / N)
            yn = yc * lax.rsqrt(var + GN_EPS) * lng_ref[:, ps] + lnb_ref[:, ps]
            bonus = seg_sum(r[p] * km[p] * rkp_ref[:, ps]) * v_ref[sl, ps]
            o_ref[sl, ps] = ((yn + bonus) * g_ref[sl, ps]).astype(o_ref.dtype)
        return 0

    lax.fori_loop(0, n_chunks, chunk, 0)


def _wkv(rkv, low, kkp, kap, rkp, lng, lnb, batch, seq, *, hw=2048, tb=256):
    _, T, D = rkv.shape
    nt = seq // tb
    slot = lambda c: pl.BlockSpec((None, tb, hw), lambda b, j, t: (c, b * nt + t, j))
    par = pl.BlockSpec((1, hw), lambda b, j, t: (0, j))
    return pl.pallas_call(
        functools.partial(_wkv_kernel, n_chunks=tb // WKV_CHUNK),
        grid=(batch, D // hw, nt),
        in_specs=[slot(c) for c in range(3)] * 2 + [par] * 5,
        out_specs=pl.BlockSpec((tb, hw), lambda b, j, t: (b * nt + t, j)),
        out_shape=jax.ShapeDtypeStruct((T, D), BF16),
        scratch_shapes=[pltpu.VMEM((hw // (2 * RW_HEAD), 2 * RW_HEAD, 2 * RW_HEAD), F32)],
        compiler_params=_params("parallel", "parallel", "arbitrary"),
        name="wkv7",
    )(rkv, rkv, rkv, low, low, low, kkp.reshape(1, D), kap.reshape(1, D), rkp.reshape(1, D),
      lng.reshape(1, D), lnb.reshape(1, D))


def _cast_kernel(w_ref, o_ref):
    o_ref[...] = w_ref[...].astype(o_ref.dtype)


CAST_BLOCK_BYTES = 8 * 1024 * 1024


def _layer_bf16(w, layer):
    _, R, C = w.shape
    br = min(R, max(16, CAST_BLOCK_BYTES // (4 * C)))
    return pl.pallas_call(
        _cast_kernel,
        grid=(R // br,),
        in_specs=[pl.BlockSpec((None, br, C), lambda r: (layer, r, 0))],
        out_specs=pl.BlockSpec((br, C), lambda r: (r, 0)),
        out_shape=jax.ShapeDtypeStruct((R, C), BF16),
        compiler_params=_params("parallel"),
        name="cast_bf16",
    )(w)


def _stack_low_rank(pairs):
    rank = max(w1.shape[1] for w1, _ in pairs)
    rank += (-rank) % LANES
    l1 = jnp.stack([jnp.pad(w1, ((0, 0), (0, rank - w1.shape[1]))) for w1, _ in pairs])
    l2 = jnp.stack([jnp.pad(w2, ((0, rank - w2.shape[0]), (0, 0))) for _, w2 in pairs])
    return l1.astype(BF16), l2.astype(BF16)


def kernel(x, g_pre_mix, g_post_mix, g_pre_ffn, g_post_ffn, ffn_w1, ffn_w2, da_wq, da_wk, da_wv, da_wo, da_lambda, da_subln, rw_mix, rw_wr, rw_wk, rw_wv, rw_wo, rw_w0, rw_w1, rw_w2, rw_a0, rw_a1, rw_a2, rw_g1, rw_g2, rw_kk, rw_ka, rw_rk, rw_lnx_g, rw_lnx_b):
    B, S, D = x.shape
    depth = g_pre_mix.shape[0]
    h = x.reshape(B * S, D)
    tabs = _rope_tables(S)
    for i in range(depth):
        j = i // N_MIXERS
        if i % N_MIXERS == 0:
            lambda_init = 0.8 - 0.6 * math.exp(-0.3 * i)
            wqkv = jnp.concatenate([_layer_bf16(w, j) for w in (da_wq, da_wk, da_wv)], axis=1)
            qkv = _qkv_proj(h, g_pre_mix[i], wqkv, tabs, S)
            att = _diff_attention(qkv, da_lambda[j], da_subln[j], lambda_init, B, S)
            h = _proj_norm_res(att, _layer_bf16(da_wo, j), g_post_mix[i], h)
        else:
            mix = rw_mix[j][jnp.array([0, 2, 3, 1, 4, 5])]
            g_in = g_pre_mix[i][None, :]
            wrkv = jnp.stack([_layer_bf16(w, j) for w in (rw_wr, rw_wk, rw_wv)])
            l1, l2 = _stack_low_rank([(rw_w1[j], rw_w2[j]), (rw_a1[j], rw_a2[j]), (rw_g1[j], rw_g2[j])])
            lb = jnp.stack([rw_w0[j], rw_a0[j], jnp.zeros_like(rw_a0[j])])[:, None, :]
            mixed, low = _rwkv_pre(h, g_in * (1.0 - mix), g_in * mix, l1, l2, lb, S)
            y = _wkv(_bmm(mixed, wrkv), low, rw_kk[j], rw_ka[j], rw_rk[j].reshape(-1), rw_lnx_g[j], rw_lnx_b[j], B, S)
            h = _proj_norm_res(y, _layer_bf16(rw_wo, j), g_post_mix[i], h)
        h = _ffn(h, g_pre_ffn[i], _layer_bf16(ffn_w1, i), _layer_bf16(ffn_w2, i), g_post_ffn[i])
    return h.reshape(B, S, D)
```

```python
import functools
import math

import jax
import jax.numpy as jnp
from jax import lax
from jax.experimental import pallas as pl
from jax.experimental.pallas import tpu as pltpu

F32 = jnp.float32
BF16 = jnp.bfloat16

CHUNK = 64
DA_HEADS = 8
DA_HEAD_DIM = 128
DA_V_DIM = 2 * DA_HEAD_DIM
ROT_DIM = DA_HEAD_DIM // 4
ROPE_THETA = 500000.0
DA_SUBLN_EPS = 1e-5
RW_HEAD = 64
GN_EPS = 64e-5
EPS = 1e-6
NEG_INF = -1e30
N_MIXERS = 2

VMEM_LIMIT_BYTES = 52 * 1024 * 1024
LANES = 128
WKV_CHUNK = 64


def _params(*sem):
    return pltpu.CompilerParams(dimension_semantics=sem, vmem_limit_bytes=VMEM_LIMIT_BYTES)


def _rms(x, eps):
    return x * lax.rsqrt(jnp.mean(x * x, axis=-1, keepdims=True) + eps)


def _dot(a, b):
    return jnp.dot(a, b, preferred_element_type=F32)


def _dot_nt(a, b):
    return lax.dot_general(a, b, (((1,), (1,)), ((), ())), preferred_element_type=F32)


def _dot_tn(a, b):
    return lax.dot_general(a, b, (((0,), (0,)), ((), ())), preferred_element_type=F32)


def _qkv_kernel(x_ref, g_ref, w_ref, tab_ref, o_ref, a_scr, *, n_sub):
    @pl.when(pl.program_id(1) == 0)
    def _():
        a_scr[...] = (_rms(x_ref[...], EPS) * g_ref[...]).astype(BF16)

    c, s1, s2 = tab_ref[0], tab_ref[1], tab_ref[2]
    half = ROT_DIM // 2
    for n in range(o_ref.shape[1] // n_sub):
        acc = _dot(a_scr[...], w_ref[:, n * n_sub:(n + 1) * n_sub])
        for h in range(n_sub // DA_HEAD_DIM):
            blk = acc[:, h * DA_HEAD_DIM:(h + 1) * DA_HEAD_DIM]
            up = pltpu.roll(blk, DA_HEAD_DIM - half, axis=1)
            dn = pltpu.roll(blk, half, axis=1)
            lanes = slice(n * n_sub + h * DA_HEAD_DIM, n * n_sub + (h + 1) * DA_HEAD_DIM)
            o_ref[:, lanes] = (blk * c + up * s1 + dn * s2).astype(o_ref.dtype)


def _qkv_proj(x, g, w, tabs, seq, *, tm=1024, tn=2048, n_sub=256):
    T, D = x.shape
    N = w.shape[1]
    n_q_tiles = (N // 3) // tn
    tiles_per_seq = seq // tm
    return pl.pallas_call(
        functools.partial(_qkv_kernel, n_sub=n_sub),
        grid=(T // tm, N // tn),
        in_specs=[
            pl.BlockSpec((tm, D), lambda i, j: (i, 0)),
            pl.BlockSpec((1, D), lambda i, j: (0, 0)),
            pl.BlockSpec((D, tn), lambda i, j: (0, j)),
            pl.BlockSpec((None, 3, tm, DA_HEAD_DIM),
                         lambda i, j: (j // n_q_tiles, 0, i % tiles_per_seq, 0)),
        ],
        out_specs=pl.BlockSpec((tm, tn), lambda i, j: (i, j)),
        out_shape=jax.ShapeDtypeStruct((T, N), BF16),
        scratch_shapes=[pltpu.VMEM((tm, D), BF16)],
        compiler_params=_params("parallel", "arbitrary"),
        name="qkv_proj",
    )(x, g.reshape(1, D), w, tabs)


def _rope_tables(seq):
    half = ROT_DIM // 2
    inv_freq = ROPE_THETA ** (-jnp.arange(half, dtype=F32) * (2.0 / ROT_DIM))
    ang = jnp.arange(seq, dtype=F32)[:, None] * inv_freq[None, :]
    cos, sin = jnp.cos(ang), jnp.sin(ang)
    rest = DA_HEAD_DIM - ROT_DIM
    c = jnp.concatenate([cos, cos, jnp.ones((seq, rest), F32)], axis=1)
    s1 = jnp.concatenate([-sin, jnp.zeros((seq, DA_HEAD_DIM - half), F32)], axis=1)
    s2 = jnp.concatenate([jnp.zeros((seq, half), F32), sin, jnp.zeros((seq, rest), F32)], axis=1)
    k_tab = jnp.stack([c, s1, s2])
    v_tab = jnp.stack([jnp.ones_like(c), jnp.zeros_like(c), jnp.zeros_like(c)])
    return jnp.stack([k_tab * (DA_HEAD_DIM ** -0.5), k_tab, v_tab])


def _attn_kernel(lam_ref, q_ref, k_ref, v_ref, g_ref, o_ref, *, blk, lambda_init):
    row = lax.broadcasted_iota(jnp.int32, (blk, blk), 0)
    col = lax.broadcasted_iota(jnp.int32, (blk, blk), 1)
    allowed = (col // CHUNK) <= (row // CHUNK)
    lv = lam_ref[...]
    lam = (jnp.exp(jnp.sum(lv[0:1] * lv[1:2], axis=-1, keepdims=True))
           - jnp.exp(jnp.sum(lv[2:3] * lv[3:4], axis=-1, keepdims=True)) + lambda_init)

    def scores(m, n):
        hs = slice(m * DA_HEAD_DIM, (m + 1) * DA_HEAD_DIM)
        past = n * blk
        q = q_ref[past:past + blk, hs]
        s_d = jnp.where(allowed, _dot_nt(q, k_ref[past:past + blk, hs]), NEG_INF)
        s_p = _dot_nt(q, k_ref[0:past, hs]) if n > 0 else None
        return s_d, s_p

    def softmax_pv(n, s_d, s_p):
        past = n * blk
        mx = jnp.max(s_d, axis=-1, keepdims=True)
        if n > 0:
            mx = jnp.maximum(mx, jnp.max(s_p, axis=-1, keepdims=True))
        p_d = jnp.exp(s_d - mx)
        l = jnp.sum(p_d, axis=-1, keepdims=True)
        acc = _dot(p_d.astype(BF16), v_ref[past:past + blk, :])
        if n > 0:
            p_p = jnp.exp(s_p - mx)
            l = l + jnp.sum(p_p, axis=-1, keepdims=True)
            acc = acc + _dot(p_p.astype(BF16), v_ref[0:past, :])
        return acc * (1.0 / l)

    n_blocks = k_ref.shape[0] // blk
    s_next = [scores(m, 0) for m in range(2)]
    for n in range(n_blocks):
        s = s_next
        if n + 1 < n_blocks:
            s_next = [scores(m, n + 1) for m in range(2)]
        o = softmax_pv(n, *s[0]) - lam * softmax_pv(n, *s[1])
        o = _rms(o, DA_SUBLN_EPS) * g_ref[...] * (1.0 - lambda_init)
        o_ref[n * blk:(n + 1) * blk, :] = o.astype(o_ref.dtype)


def _diff_attention(qkv, lam_vecs, subln_g, lambda_init, batch, seq, *, blk=256):
    T = qkv.shape[0]
    return pl.pallas_call(
        functools.partial(_attn_kernel, blk=blk, lambda_init=lambda_init),
        grid=(batch, DA_HEADS),
        in_specs=[
            pl.BlockSpec((4, DA_HEAD_DIM), lambda b, h: (0, 0)),
            pl.BlockSpec((seq, DA_V_DIM), lambda b, h: (b, h)),
            pl.BlockSpec((seq, DA_V_DIM), lambda b, h: (b, DA_HEADS + h)),
            pl.BlockSpec((seq, DA_V_DIM), lambda b, h: (b, 2 * DA_HEADS + h)),
            pl.BlockSpec((1, DA_V_DIM), lambda b, h: (0, 0)),
        ],
        out_specs=pl.BlockSpec((seq, DA_V_DIM), lambda b, h: (b, h)),
        out_shape=jax.ShapeDtypeStruct((T, DA_HEADS * DA_V_DIM), BF16),
        compiler_params=_params("parallel", "parallel"),
        name="diff_attn",
    )(lam_vecs, qkv, qkv, qkv, subln_g.reshape(1, DA_V_DIM))


def _proj_norm_res_kernel(a_ref, w_ref, g_ref, x_ref, o_ref):
    m = _dot(a_ref[...], w_ref[...])
    o_ref[...] = x_ref[...] + _rms(m, EPS) * g_ref[...]


def _proj_norm_res(a, w, g, resid, *, tm=512):
    T, K = a.shape
    N = w.shape[1]
    return pl.pallas_call(
        _proj_norm_res_kernel,
        grid=(T // tm,),
        in_specs=[
            pl.BlockSpec((tm, K), lambda i: (i, 0)),
            pl.BlockSpec((K, N), lambda i: (0, 0)),
            pl.BlockSpec((1, N), lambda i: (0, 0)),
            pl.BlockSpec((tm, N), lambda i: (i, 0)),
        ],
        out_specs=pl.BlockSpec((tm, N), lambda i: (i, 0)),
        out_shape=jax.ShapeDtypeStruct((T, N), F32),
        compiler_params=_params("parallel"),
        name="proj_norm_res",
    )(a, w, g.reshape(1, N), resid)


def _ffn_kernel(x_ref, g1_ref, w1_ref, w2_ref, g2_ref, o_ref, h_scr, acc_scr):
    f = pl.program_id(1)

    @pl.when(f == 0)
    def _():
        h_scr[...] = (_rms(x_ref[...], EPS) * g1_ref[...]).astype(BF16)
        acc_scr[...] = jnp.zeros_like(acc_scr)

    u = jnp.maximum(_dot(h_scr[...], w1_ref[...]), 0.0)
    acc_scr[...] += _dot((u * u).astype(BF16), w2_ref[...])

    @pl.when(f == pl.num_programs(1) - 1)
    def _():
        o_ref[...] = x_ref[...] + _rms(acc_scr[...], EPS) * g2_ref[...]


def _ffn(x, g1, w1, w2, g2, *, tm=512, tf=1024):
    T, D = x.shape
    F = w1.shape[1]
    return pl.pallas_call(
        _ffn_kernel,
        grid=(T // tm, F // tf),
        in_specs=[
            pl.BlockSpec((tm, D), lambda i, f: (i, 0)),
            pl.BlockSpec((1, D), lambda i, f: (0, 0)),
            pl.BlockSpec((D, tf), lambda i, f: (0, f)),
            pl.BlockSpec((tf, D), lambda i, f: (f, 0)),
            pl.BlockSpec((1, D), lambda i, f: (0, 0)),
        ],
        out_specs=pl.BlockSpec((tm, D), lambda i, f: (i, 0)),
        out_shape=jax.ShapeDtypeStruct((T, D), F32),
        scratch_shapes=[pltpu.VMEM((tm, D), BF16), pltpu.VMEM((tm, D), F32)],
        compiler_params=_params("parallel", "arbitrary"),
        name="ffn",
    )(x, g1.reshape(1, D), w1, w2, g2.reshape(1, D))


def _rwkv_pre_kernel(x_ref, xp_ref, u_ref, w_ref, l1_ref, l2_ref, lb_ref, mixed_ref, low_ref, *, tiles_per_seq):
    i = pl.program_id(0)
    n_dense = mixed_ref.shape[0]
    xs = _rms(x_ref[...], EPS)
    sub = xp_ref.shape[0]
    xp = _rms(xp_ref[sub - 1:sub, :], EPS)
    xp = jnp.where(i % tiles_per_seq == 0, jnp.zeros_like(xp), xp)
    row = lax.broadcasted_iota(jnp.int32, (xs.shape[0], 1), 0)
    prev = jnp.where(row == 0, xp, pltpu.roll(xs, 1, axis=0))

    def mixed(c):
        return (xs * u_ref[c:c + 1, :] + prev * w_ref[c:c + 1, :]).astype(BF16)

    def sigmoid(z):
        return 0.5 * jnp.tanh(0.5 * z) + 0.5

    for c in range(n_dense):
        mixed_ref[c] = mixed(c)
    pre = (jnp.tanh, lambda t: t, sigmoid)
    post = (lambda z: -math.exp(-0.5) * sigmoid(z + lb_ref[0]), lambda z: sigmoid(z + lb_ref[1]), lambda z: z)
    t = [_dot(mixed(n_dense + l), l1_ref[l]) for l in range(3)]
    t = [pre[l](t[l]).astype(BF16) for l in range(3)]
    z = [_dot(t[l], l2_ref[l]) for l in range(3)]
    for l in range(3):
        low_ref[l] = post[l](z[l])


def _rwkv_pre(x, u, w, l1, l2, lb, seq, *, tm=256, sub=8):
    T, D = x.shape
    n_proj = u.shape[0]
    n_low, _, R = l1.shape
    n_dense = n_proj - n_low
    per = tm // sub
    whole = lambda shape: pl.BlockSpec(shape, lambda i: (0,) * len(shape))
    return pl.pallas_call(
        functools.partial(_rwkv_pre_kernel, tiles_per_seq=seq // tm),
        grid=(T // tm,),
        in_specs=[
            pl.BlockSpec((tm, D), lambda i: (i, 0)),
            pl.BlockSpec((sub, D), lambda i: (jnp.maximum(i * per - 1, 0), 0)),
            whole((n_proj, D)), whole((n_proj, D)),
            whole((n_low, D, R)), whole((n_low, R, D)), whole((n_low, 1, D)),
        ],
        out_specs=[pl.BlockSpec((n_dense, tm, D), lambda i: (0, i, 0)),
                   pl.BlockSpec((n_low, tm, D), lambda i: (0, i, 0))],
        out_shape=[jax.ShapeDtypeStruct((n_dense, T, D), BF16), jax.ShapeDtypeStruct((n_low, T, D), F32)],
        compiler_params=_params("parallel"),
        name="rwkv_pre",
    )(x, x, u, w, l1, l2, lb)


def _bmm_kernel(a_ref, w_ref, o_ref):
    o_ref[...] = _dot(a_ref[...], w_ref[...])


def _bmm(a, w, *, tm=1024, tn=2048):
    P, K, N = w.shape
    T = a.shape[1]
    return pl.pallas_call(
        _bmm_kernel,
        grid=(P, T // tm, N // tn),
        in_specs=[
            pl.BlockSpec((None, tm, K), lambda p, i, j: (p, i, 0)),
            pl.BlockSpec((None, K, tn), lambda p, i, j: (p, 0, j)),
        ],
        out_specs=pl.BlockSpec((None, tm, tn), lambda p, i, j: (p, i, j)),
        out_shape=jax.ShapeDtypeStruct((P, T, N), F32),
        compiler_params=_params("parallel", "parallel", "arbitrary"),
        name="rkv_proj",
    )(a, w)


def _cumsum_rows(tri, x):
    hi = x.astype(BF16)
    r1 = x - hi.astype(F32)
    mid = r1.astype(BF16)
    lo = (r1 - mid.astype(F32)).astype(BF16)
    return _dot(tri, hi) + _dot(tri, mid) + _dot(tri, lo)


def _wkv_kernel(r_ref, k_ref, v_ref, lw_ref, a_ref, g_ref, kkp_ref, kap_ref, rkp_ref, lng_ref, lnb_ref,
                o_ref, s_scr, *, n_chunks):
    C = WKV_CHUNK
    N = RW_HEAD
    PW = 2 * N
    pairs = s_scr.shape[0]

    @pl.when(pl.program_id(2) == 0)
    def _():
        s_scr[...] = jnp.zeros_like(s_scr)

    row = lax.broadcasted_iota(jnp.int32, (C, PW), 0)
    lane = lax.broadcasted_iota(jnp.int32, (C, PW), 1)
    col = lane % N
    lo = lane < N
    strict = row > col
    incl = row >= col
    incl2 = jnp.concatenate([incl, incl], axis=1)
    eye = (row == col).astype(F32)
    tri = (lax.broadcasted_iota(jnp.int32, (C, C), 0) >= lax.broadcasted_iota(jnp.int32, (C, C), 1)).astype(BF16)
    base_f = ((row // 2) == (col // 2)).astype(F32)
    level_f = []
    s = 2
    while s < C:
        level_f.append((((row // (2 * s)) == (col // (2 * s))) & ((row // s) != (col // s))).astype(F32))
        s *= 2
    diag_blocks = ((lax.broadcasted_iota(jnp.int32, (PW, PW), 0) < N)
                   == (lax.broadcasted_iota(jnp.int32, (PW, PW), 1) < N))

    def bd(z):
        zero = jnp.zeros_like(z)
        return jnp.concatenate([jnp.where(lo, z, zero), jnp.where(lo, zero, z)], axis=0)

    def seg_sum(x):
        s_lo = jnp.sum(jnp.where(lo, x, 0.0), axis=-1, keepdims=True)
        s_hi = jnp.sum(jnp.where(lo, 0.0, x), axis=-1, keepdims=True)
        return jnp.where(lo, s_lo, s_hi)

    def chunk(c, _):
        t0 = pl.multiple_of(c * C, C)
        sl = pl.ds(t0, C)
        lw_all = lw_ref[sl, :]
        L_all = _cumsum_rows(tri, lw_all)
        psl = [slice(p * PW, (p + 1) * PW) for p in range(pairs)]
        P = range(pairs)
        r = [r_ref[sl, ps] for ps in psl]
        k = [k_ref[sl, ps] for ps in psl]
        a = [a_ref[sl, ps] for ps in psl]
        vb = [v_ref[sl, ps].astype(BF16) for ps in psl]
        L = [L_all[:, ps] for ps in psl]
        Lc = [x[C - 1:C, :] for x in L]
        km = [k[p] * (1.0 + (a[p] - 1.0) * kap_ref[:, psl[p]]) for p in P]
        kkr = [k[p] * kkp_ref[:, psl[p]] for p in P]
        kk = [kkr[p] * lax.rsqrt(jnp.maximum(seg_sum(kkr[p] * kkr[p]), 1e-24)) for p in P]
        b = [kk[p] * a[p] for p in P]
        e_nl = [jnp.exp(-L[p]) for p in P]
        e_end = [jnp.exp(Lc[p] - L[p]) for p in P]
        lhs = [jnp.concatenate([-kk[p] * jnp.exp(L[p] - lw_all[:, psl[p]]), r[p] * jnp.exp(L[p])],
                               axis=0).astype(BF16) for p in P]
        rhs = [jnp.concatenate([bd((b[p] * e_nl[p]).astype(BF16)), bd((km[p] * e_nl[p]).astype(BF16))],
                               axis=0) for p in P]
        tn_rhs = [jnp.concatenate([(b[p] * e_end[p]).astype(BF16), (km[p] * e_end[p]).astype(BF16)], axis=0)
                  for p in P]
        G = [_dot_nt(lhs[p], rhs[p]) for p in P]
        S0 = [s_scr[p] for p in P]
        SA = [_dot_nt(lhs[p], S0[p].astype(BF16)) for p in P]
        A_ab = [jnp.where(strict, G[p][:C, :PW], 0.0) for p in P]
        A_ak = [jnp.where(strict, G[p][:C, PW:], 0.0).astype(BF16) for p in P]
        P_rbk = [jnp.where(incl2, G[p][C:, :], 0.0).astype(BF16) for p in P]
        vbd = [bd(vb[p]) for p in P]
        W = [SA[p][:C] + _dot(A_ak[p], vbd[p]) for p in P]
        X = [eye + A_ab[p] * base_f for p in P]
        for lvl in level_f:
            Xb = [X[p].astype(BF16) for p in P]
            inner = [_dot((A_ab[p] * lvl).astype(BF16), bd(Xb[p])).astype(BF16) for p in P]
            X = [X[p] + _dot(Xb[p], bd(inner[p])) for p in P]
        ub = [_dot(X[p].astype(BF16), bd(W[p].astype(BF16))).astype(BF16) for p in P]
        Y = [SA[p][C:] + _dot(P_rbk[p], jnp.concatenate([bd(ub[p]), vbd[p]], axis=0)) for p in P]
        for p in P:
            upd = _dot_tn(jnp.concatenate([ub[p], vb[p]], axis=0), tn_rhs[p])
            s_scr[p] = jnp.where(diag_blocks, S0[p] * jnp.exp(Lc[p]) + upd, 0.0)
        for p in P:
            ps = psl[p]
            yc = Y[p] - seg_sum(Y[p]) * (1.0 / N)
            var = seg_sum(yc * yc) * (1.0 / N)
            yn = yc * lax.rsqrt(var + GN_EPS) * lng_ref[:, ps] + lnb_ref[:, ps]
            bonus = seg_sum(r[p] * km[p] * rkp_ref[:, ps]) * v_ref[sl, ps]
            o_ref[sl, ps] = ((yn + bonus) * g_ref[sl, ps]).astype(o_ref.dtype)
        return 0

    for c in range(n_chunks):
        chunk(c, 0)


def _wkv(rkv, low, kkp, kap, rkp, lng, lnb, batch, seq, *, hw=2048, tb=256):
    _, T, D = rkv.shape
    nt = seq // tb
    slot = lambda c: pl.BlockSpec((None, tb, hw), lambda b, j, t: (c, b * nt + t, j))
    par = pl.BlockSpec((1, hw), lambda b, j, t: (0, j))
    return pl.pallas_call(
        functools.partial(_wkv_kernel, n_chunks=tb // WKV_CHUNK),
        grid=(batch, D // hw, nt),
        in_specs=[slot(c) for c in range(3)] * 2 + [par] * 5,
        out_specs=pl.BlockSpec((tb, hw), lambda b, j, t: (b * nt + t, j)),
        out_shape=jax.ShapeDtypeStruct((T, D), BF16),
        scratch_shapes=[pltpu.VMEM((hw // (2 * RW_HEAD), 2 * RW_HEAD, 2 * RW_HEAD), F32)],
        compiler_params=_params("parallel", "parallel", "arbitrary"),
        name="wkv7",
    )(rkv, rkv, rkv, low, low, low, kkp.reshape(1, D), kap.reshape(1, D), rkp.reshape(1, D),
      lng.reshape(1, D), lnb.reshape(1, D))


def _cast_kernel(w_ref, o_ref):
    o_ref[...] = w_ref[...].astype(o_ref.dtype)


CAST_BLOCK_BYTES = 8 * 1024 * 1024


def _layer_bf16(w, layer):
    _, R, C = w.shape
    br = min(R, max(16, CAST_BLOCK_BYTES // (4 * C)))
    return pl.pallas_call(
        _cast_kernel,
        grid=(R // br,),
        in_specs=[pl.BlockSpec((None, br, C), lambda r: (layer, r, 0))],
        out_specs=pl.BlockSpec((br, C), lambda r: (r, 0)),
        out_shape=jax.ShapeDtypeStruct((R, C), BF16),
        compiler_params=_params("parallel"),
        name="cast_bf16",
    )(w)


def _stack_low_rank(pairs):
    rank = max(w1.shape[1] for w1, _ in pairs)
    rank += (-rank) % LANES
    l1 = jnp.stack([jnp.pad(w1, ((0, 0), (0, rank - w1.shape[1]))) for w1, _ in pairs])
    l2 = jnp.stack([jnp.pad(w2, ((0, rank - w2.shape[0]), (0, 0))) for _, w2 in pairs])
    return l1.astype(BF16), l2.astype(BF16)


def kernel(x, g_pre_mix, g_post_mix, g_pre_ffn, g_post_ffn, ffn_w1, ffn_w2, da_wq, da_wk, da_wv, da_wo, da_lambda, da_subln, rw_mix, rw_wr, rw_wk, rw_wv, rw_wo, rw_w0, rw_w1, rw_w2, rw_a0, rw_a1, rw_a2, rw_g1, rw_g2, rw_kk, rw_ka, rw_rk, rw_lnx_g, rw_lnx_b):
    B, S, D = x.shape
    depth = g_pre_mix.shape[0]
    h = x.reshape(B * S, D)
    tabs = _rope_tables(S)
    for i in range(depth):
        j = i // N_MIXERS
        if i % N_MIXERS == 0:
            lambda_init = 0.8 - 0.6 * math.exp(-0.3 * i)
            wqkv = jnp.concatenate([_layer_bf16(w, j) for w in (da_wq, da_wk, da_wv)], axis=1)
            qkv = _qkv_proj(h, g_pre_mix[i], wqkv, tabs, S)
            att = _diff_attention(qkv, da_lambda[j], da_subln[j], lambda_init, B, S)
            h = _proj_norm_res(att, _layer_bf16(da_wo, j), g_post_mix[i], h)
        else:
            mix = rw_mix[j][jnp.array([0, 2, 3, 1, 4, 5])]
            g_in = g_pre_mix[i][None, :]
            wrkv = jnp.stack([_layer_bf16(w, j) for w in (rw_wr, rw_wk, rw_wv)])
            l1, l2 = _stack_low_rank([(rw_w1[j], rw_w2[j]), (rw_a1[j], rw_a2[j]), (rw_g1[j], rw_g2[j])])
            lb = jnp.stack([rw_w0[j], rw_a0[j], jnp.zeros_like(rw_a0[j])])[:, None, :]
            mixed, low = _rwkv_pre(h, g_in * (1.0 - mix), g_in * mix, l1, l2, lb, S)
            y = _wkv(_bmm(mixed, wrkv), low, rw_kk[j], rw_ka[j], rw_rk[j].reshape(-1), rw_lnx_g[j], rw_lnx_b[j], B, S)
            h = _proj_norm_res(y, _layer_bf16(rw_wo, j), g_post_mix[i], h)
        h = _ffn(h, g_pre_ffn[i], _layer_bf16(ffn_w1, i), _layer_bf16(ffn_w2, i), g_post_ffn[i])
    return h.reshape(B, S, D)
```
